```python
import jax, jax.numpy as jnp
from jax import lax
import numpy as np

D_MODEL = 4096
BATCH = 4
SEQ = 2048
DEPTH = 2
DEC_BATCH = 8
DEC_SEQ = 1
PAST_LEN = 16384
PAGE_SIZE = 128

HEAD_DIM = 128
H_A = 8
KV_A = 4
H_B = 12
KV_B = 6
H_C = 12
KV_C = 6
W_A = H_A * HEAD_DIM
W_B = H_B * HEAD_DIM
W_C = H_C * HEAD_DIM
N_BRANCH = 3
Q_BLOCK = 128
MOBA_BLOCK = 256
MOBA_TOPK = 3
MOBA_Q_CHUNK = 16
ALIBI_MAX_BIAS = 8.0
N_GROUPS = 4
E_PER_GROUP = 4
N_EXPERTS = N_GROUPS * E_PER_GROUP
TOPK_IN_GROUP = 2
D_FF = D_MODEL // 4
RMS_EPS = 1e-6
IN_SPLITS = (W_A, KV_A * HEAD_DIM, KV_A * HEAD_DIM,
             W_B, KV_B * HEAD_DIM, KV_B * HEAD_DIM, H_B,
             W_C, KV_C * HEAD_DIM, KV_C * HEAD_DIM,
             N_BRANCH * D_MODEL)
N_IN = sum(IN_SPLITS)

kernel_name = 'hybrid_sb_fox_moba_hmoe_step'


def rmsnorm(x, g):
    xf = x.astype(jnp.float32)
    y = xf * lax.rsqrt(jnp.mean(xf * xf, axis=-1, keepdims=True) + RMS_EPS)
    return (y * g.astype(jnp.float32)).astype(x.dtype)


def _block_size(tq, pref):
    return pref if tq % pref == 0 else tq


def _sweep(fn, blk, tpos, *arrs):
    n = tpos.shape[0] // blk
    def split(a):
        return jnp.moveaxis(a.reshape((a.shape[0], n, blk) + a.shape[2:]), 1, 0)
    xs = (tpos.reshape(n, blk),) + tuple(split(a) for a in arrs)
    out = lax.map(lambda c: fn(*c), xs)
    out = jnp.moveaxis(out, 0, 1)
    return out.reshape((out.shape[0], n * blk) + out.shape[3:])


def stick_breaking_attention(q, k, v, pos0):
    B, Tq, H, Dh = q.shape
    L, KV = k.shape[1], k.shape[2]
    G = H // KV
    spos = jnp.arange(L)
    scale = Dh ** -0.5
    def blockfn(tpos, qb):
        qg = qb.reshape(B, -1, KV, G, Dh)
        z = jnp.einsum('bqhgd,bshd->bhgqs', qg, k).astype(jnp.float32) * scale
        reads = spos[None, :] < tpos[:, None]
        sp = jnp.where(reads, jax.nn.softplus(z), 0.0)
        later = lax.cumsum(sp, axis=4, reverse=True) - sp
        w = jnp.where(reads, jnp.exp(jax.nn.log_sigmoid(z) - later), 0.0)
        o = jnp.einsum('bhgqs,bshd->bqhgd', w.astype(v.dtype), v)
        return o.reshape(B, -1, H, Dh)
    tpos = pos0 + jnp.arange(Tq)
    return _sweep(blockfn, _block_size(Tq, Q_BLOCK), tpos, q)


def forgetting_attention(q, k, v, logf, pos0):
    B, Tq, H, Dh = q.shape
    L, KV = k.shape[1], k.shape[2]
    G = H // KV
    spos = jnp.arange(L)
    scale = Dh ** -0.5
    cum = lax.cumsum(logf.astype(jnp.float32), axis=1)
    cq = cum[:, pos0:]
    ck = cum.reshape(B, L, KV, G).transpose(0, 2, 3, 1)
    def blockfn(tpos, qb, cqb):
        qg = qb.reshape(B, -1, KV, G, Dh)
        z = jnp.einsum('bqhgd,bshd->bhgqs', qg, k).astype(jnp.float32) * scale
        dq = cqb.reshape(B, -1, KV, G).transpose(0, 2, 3, 1)
        z = z + dq[..., None] - ck[:, :, :, None, :]
        mask = spos[None, :] <= tpos[:, None]
        p = jax.nn.softmax(jnp.where(mask, z, -jnp.inf), axis=-1)
        o = jnp.einsum('bhgqs,bshd->bqhgd', p.astype(v.dtype), v)
        return o.reshape(B, -1, H, Dh)
    tpos = pos0 + jnp.arange(Tq)
    return _sweep(blockfn, _block_size(Tq, Q_BLOCK), tpos, q, cq)


def moba_attention(q, k, v, pos0, slopes):
    B, Tq, H, Dh = q.shape
    L, KV = k.shape[1], k.shape[2]
    G = H // KV
    scale = Dh ** -0.5
    nb = -(-L // MOBA_BLOCK)
    pad = nb * MOBA_BLOCK - L
    kb = jnp.pad(k, ((0, 0), (0, pad), (0, 0), (0, 0))).reshape(B, nb, MOBA_BLOCK, KV, Dh)
    vb = jnp.pad(v, ((0, 0), (0, pad), (0, 0), (0, 0))).reshape(B, nb, MOBA_BLOCK, KV, Dh)
    kmean = jnp.mean(kb.astype(jnp.float32), axis=2).astype(q.dtype)
    n_sel = min(MOBA_TOPK, nb)
    m = slopes.reshape(KV, G)
    bi = jnp.arange(B)[:, None, None, None, None]
    hi = jnp.arange(KV)[None, None, :, None, None]
    offs = jnp.arange(MOBA_BLOCK)
    def blockfn(tpos, qb):
        qg = qb.reshape(B, -1, KV, G, Dh)
        bt = tpos // MOBA_BLOCK
        gate = jnp.einsum('bqhgd,bnhd->bqhgn', qg, kmean).astype(jnp.float32)
        past_blk = jnp.arange(nb)[None, :] < bt[:, None]
        gate = jnp.where(past_blk[None, :, None, None, :], gate, -jnp.inf)
        _, sel = lax.top_k(gate, n_sel)
        sel_ok = sel < bt[None, :, None, None, None]
        ks = kb[bi, sel, :, hi, :]
        vs = vb[bi, sel, :, hi, :]
        s_sel = jnp.einsum('bqhgd,bqhgjsd->bqhgjs', qg, ks).astype(jnp.float32) * scale
        dist_sel = tpos[None, :, None, None, None, None] - (sel[..., None] * MOBA_BLOCK + offs)
        s_sel = jnp.where(sel_ok[..., None],
                          s_sel - m[None, None, :, :, None, None] * dist_sel, -jnp.inf)
        own_k = kb[jnp.arange(B)[:, None], bt[None, :]]
        own_v = vb[jnp.arange(B)[:, None], bt[None, :]]
        s_own = jnp.einsum('bqhgd,bqshd->bqhgs', qg, own_k).astype(jnp.float32) * scale
        dist_own = tpos[:, None] - (bt[:, None] * MOBA_BLOCK + offs)
        d_own = dist_own[None, :, None, None, :]
        s_own = jnp.where(d_own >= 0, s_own - m[None, None, :, :, None] * d_own, -jnp.inf)
        qc = qg.shape[1]
        s = jnp.concatenate([s_sel.reshape(B, qc, KV, G, n_sel * MOBA_BLOCK), s_own], axis=-1)
        p = jax.nn.softmax(s, axis=-1).astype(v.dtype)
        p_sel = p[..., :n_sel * MOBA_BLOCK].reshape(B, qc, KV, G, n_sel, MOBA_BLOCK)
        p_own = p[..., n_sel * MOBA_BLOCK:]
        o = (jnp.einsum('bqhgjs,bqhgjsd->bqhgd', p_sel, vs)
             + jnp.einsum('bqhgs,bqshd->bqhgd', p_own, own_v))
        return o.reshape(B, qc, H, Dh)
    tpos = pos0 + jnp.arange(Tq)
    return _sweep(blockfn, _block_size(Tq, MOBA_Q_CHUNK), tpos, q)


def _split_in(u):
    idx = [int(i) for i in np.cumsum(IN_SPLITS)[:-1]]
    return jnp.split(u, idx, axis=-1)


def mixer(xn, past, w_in, b_f, w_br_a, w_br_b, w_br_c, w_o):
    B, T, _ = xn.shape
    u = xn @ w_in
    q_a, k_a, v_a, q_b, k_b, v_b, f_b, q_c, k_c, v_c, g = _split_in(u)
    hd = lambda a, h: a.reshape(B, T, h, HEAD_DIM)
    kv_a_new = jnp.stack([hd(k_a, KV_A), hd(v_a, KV_A)], axis=2)
    kv_b_new = jnp.stack([hd(k_b, KV_B), hd(v_b, KV_B)], axis=2)
    kv_c_new = jnp.stack([hd(k_c, KV_C), hd(v_c, KV_C)], axis=2)
    logf_new = jax.nn.log_sigmoid(f_b.astype(jnp.float32) + b_f.astype(jnp.float32))
    if past is None:
        kv_a, kv_b, logf, kv_c = kv_a_new, kv_b_new, logf_new, kv_c_new
    else:
        kv_a = jnp.concatenate([past[0], kv_a_new.astype(past[0].dtype)], axis=1)
        kv_b = jnp.concatenate([past[1], kv_b_new.astype(past[1].dtype)], axis=1)
        logf = jnp.concatenate([past[2].astype(jnp.float32), logf_new], axis=1)
        kv_c = jnp.concatenate([past[3], kv_c_new.astype(past[3].dtype)], axis=1)
    pos0 = kv_a.shape[1] - T
    slopes = 2.0 ** (-ALIBI_MAX_BIAS * jnp.arange(1, H_C + 1, dtype=jnp.float32) / H_C)
    o_a = stick_breaking_attention(hd(q_a, H_A), kv_a[:, :, 0], kv_a[:, :, 1], pos0)
    o_b = forgetting_attention(hd(q_b, H_B), kv_b[:, :, 0], kv_b[:, :, 1], logf, pos0)
    o_c = moba_attention(hd(q_c, H_C), kv_c[:, :, 0], kv_c[:, :, 1], pos0, slopes)
    gates = jax.nn.sigmoid(g.reshape(B, T, N_BRANCH, D_MODEL))
    merged = (gates[:, :, 0] * (o_a.reshape(B, T, W_A) @ w_br_a)
              + gates[:, :, 1] * (o_b.reshape(B, T, W_B) @ w_br_b)
              + gates[:, :, 2] * (o_c.reshape(B, T, W_C) @ w_br_c))
    return merged @ w_o, (kv_a_new, kv_b_new, logf_new, kv_c_new)


def hierarchical_moe(x, w_rg, b_rg, w_re, b_re, w_eg, w_eu, w_ed):
    B, T, D = x.shape
    xt = x.reshape(B * T, D)
    lg = (xt @ w_rg).astype(jnp.float32) + b_rg.astype(jnp.float32)
    pg = jax.nn.softmax(lg, axis=-1)
    grp = jnp.argmax(lg, axis=-1)
    p_grp = jnp.take_along_axis(pg, grp[:, None], axis=-1)
    le = ((xt @ w_re).astype(jnp.float32) + b_re.astype(jnp.float32)).reshape(-1, N_GROUPS, E_PER_GROUP)
    le = jnp.take_along_axis(le, grp[:, None, None], axis=1)[:, 0]
    top_p, top_i = lax.top_k(jax.nn.softmax(le, axis=-1), TOPK_IN_GROUP)
    top_p = top_p / jnp.sum(top_p, axis=-1, keepdims=True)
    expert = grp[:, None] * E_PER_GROUP + top_i
    comb = jnp.sum(jax.nn.one_hot(expert, N_EXPERTS, dtype=jnp.float32)
                   * (p_grp * top_p)[..., None], axis=1)
    h = jax.nn.silu(jnp.einsum('nd,edf->nef', xt, w_eg)) * jnp.einsum('nd,edf->nef', xt, w_eu)
    y = jnp.einsum('nef,efd->nd', h * comb[..., None].astype(h.dtype), w_ed)
    return y.reshape(B, T, D)


def decoder_layer(x, past, w_in, b_f, w_br_a, w_br_b, w_br_c, w_o, g_mix, g_ffn,
                  w_rg, b_rg, w_re, b_re, w_eg, w_eu, w_ed):
    a, rows = mixer(rmsnorm(x, g_mix), past, w_in, b_f, w_br_a, w_br_b, w_br_c, w_o)
    h = x + a
    y = h + hierarchical_moe(rmsnorm(h, g_ffn), w_rg, b_rg, w_re, b_re, w_eg, w_eu, w_ed)
    return y, rows


def gather_paged(cache_l, page_table):
    rows = cache_l[page_table]
    return rows.reshape((page_table.shape[0], -1) + cache_l.shape[2:])


def setup_inputs(seed: int = 0) -> dict:
    key = jax.random.key(seed)
    ks = jax.random.split(key, 24)
    n_pages = PAST_LEN // PAGE_SIZE
    n_used = DEC_BATCH * n_pages
    n_pool = n_used + max(1, n_used // 4)
    f32 = jnp.float32
    def nrm(k, shape, scale=1.0):
        return jax.random.normal(k, shape, f32) * scale
    x_prompt = nrm(ks[0], (BATCH, SEQ, D_MODEL))
    x_sample = nrm(ks[1], (DEC_BATCH, DEC_SEQ, D_MODEL))
    cache_kv_a = nrm(ks[2], (DEPTH, n_pool, PAGE_SIZE, 2, KV_A, HEAD_DIM))
    cache_kv_b = nrm(ks[3], (DEPTH, n_pool, PAGE_SIZE, 2, KV_B, HEAD_DIM))
    cache_logf_b = jax.nn.log_sigmoid(2.0 + nrm(ks[4], (DEPTH, n_pool, PAGE_SIZE, H_B)))
    cache_kv_c = nrm(ks[5], (DEPTH, n_pool, PAGE_SIZE, 2, KV_C, HEAD_DIM))
    page_table = jax.random.permutation(ks[6], n_pool)[:n_used].reshape(DEC_BATCH, n_pages).astype(jnp.int32)
    w_in = nrm(ks[7], (DEPTH, D_MODEL, N_IN), D_MODEL ** -0.5)
    b_f = 2.0 + nrm(ks[8], (DEPTH, H_B), 0.5)
    w_br_a = nrm(ks[9], (DEPTH, W_A, D_MODEL), W_A ** -0.5)
    w_br_b = nrm(ks[10], (DEPTH, W_B, D_MODEL), W_B ** -0.5)
    w_br_c = nrm(ks[11], (DEPTH, W_C, D_MODEL), W_C ** -0.5)
    w_o = nrm(ks[12], (DEPTH, D_MODEL, D_MODEL), D_MODEL ** -0.5)
    g_norm_mix = 1.0 + nrm(ks[13], (DEPTH, D_MODEL), 0.1)
    g_norm_ffn = 1.0 + nrm(ks[14], (DEPTH, D_MODEL), 0.1)
    w_router_group = nrm(ks[15], (DEPTH, D_MODEL, N_GROUPS), D_MODEL ** -0.5)
    b_router_group = nrm(ks[16], (DEPTH, N_GROUPS), 0.01)
    w_router_expert = nrm(ks[17], (DEPTH, D_MODEL, N_EXPERTS), D_MODEL ** -0.5)
    b_router_expert = nrm(ks[18], (DEPTH, N_EXPERTS), 0.01)
    w_exp_gate = nrm(ks[19], (DEPTH, N_EXPERTS, D_MODEL, D_FF), D_MODEL ** -0.5)
    w_exp_up = nrm(ks[20], (DEPTH, N_EXPERTS, D_MODEL, D_FF), D_MODEL ** -0.5)
    w_exp_down = nrm(ks[21], (DEPTH, N_EXPERTS, D_FF, D_MODEL), D_FF ** -0.5)
    g_norm_final = 1.0 + nrm(ks[22], (D_MODEL,), 0.1)
    return {'x_prompt': x_prompt, 'x_sample': x_sample,
            'cache_kv_a': cache_kv_a, 'cache_kv_b': cache_kv_b,
            'cache_logf_b': cache_logf_b, 'cache_kv_c': cache_kv_c,
            'page_table': page_table,
            'w_in': w_in, 'b_f': b_f, 'w_br_a': w_br_a, 'w_br_b': w_br_b, 'w_br_c': w_br_c,
            'w_o': w_o, 'g_norm_mix': g_norm_mix, 'g_norm_ffn': g_norm_ffn,
            'w_router_group': w_router_group, 'b_router_group': b_router_group,
            'w_router_expert': w_router_expert, 'b_router_expert': b_router_expert,
            'w_exp_gate': w_exp_gate, 'w_exp_up': w_exp_up, 'w_exp_down': w_exp_down,
            'g_norm_final': g_norm_final}


def reference(x_prompt, x_sample, cache_kv_a, cache_kv_b, cache_logf_b, cache_kv_c, page_table,
              w_in, b_f, w_br_a, w_br_b, w_br_c, w_o, g_norm_mix, g_norm_ffn,
              w_router_group, b_router_group, w_router_expert, b_router_expert,
              w_exp_gate, w_exp_up, w_exp_down, g_norm_final):
    xp, xs = x_prompt, x_sample
    rows_p, rows_s = [], []
    for l in range(DEPTH):
        lw = (w_in[l], b_f[l], w_br_a[l], w_br_b[l], w_br_c[l], w_o[l], g_norm_mix[l], g_norm_ffn[l],
              w_router_group[l], b_router_group[l], w_router_expert[l], b_router_expert[l],
              w_exp_gate[l], w_exp_up[l], w_exp_down[l])
        past = (gather_paged(cache_kv_a[l], page_table), gather_paged(cache_kv_b[l], page_table),
                gather_paged(cache_logf_b[l], page_table), gather_paged(cache_kv_c[l], page_table))
        xp, rp = decoder_layer(xp, None, *lw)
        xs, rs = decoder_layer(xs, past, *lw)
        rows_p.append(rp)
        rows_s.append(rs)
    y_prompt = rmsnorm(xp, g_norm_final)
    y_sample = rmsnorm(xs, g_norm_final)
    kv_a_prompt = jnp.stack([r[0] for r in rows_p], axis=0)
    kv_b_prompt = jnp.stack([r[1] for r in rows_p], axis=0)
    logf_b_prompt = jnp.stack([r[2] for r in rows_p], axis=0)
    kv_c_prompt = jnp.stack([r[3] for r in rows_p], axis=0)
    kv_a_sample = jnp.stack([r[0] for r in rows_s], axis=0)
    kv_b_sample = jnp.stack([r[1] for r in rows_s], axis=0)
    logf_b_sample = jnp.stack([r[2] for r in rows_s], axis=0)
    kv_c_sample = jnp.stack([r[3] for r in rows_s], axis=0)
    return (y_prompt, y_sample, kv_a_prompt, kv_b_prompt, logf_b_prompt, kv_c_prompt,
            kv_a_sample, kv_b_sample, logf_b_sample, kv_c_sample)
```

```python
import functools

import jax
import jax.numpy as jnp
from jax import lax
from jax.experimental import pallas as pl
from jax.experimental.pallas import tpu as pltpu

F32 = jnp.float32
BF16 = jnp.bfloat16

HEAD_DIM = 128
LANES = 128
MXU_COLS = 256
V7X_VMEM_BYTES = 64 * 1024 * 1024
VMEM_LIMIT_BIG = 56 * 1024 * 1024
VMEM_LIMIT_MID = 40 * 1024 * 1024
Q_TILE = 256
MOBA_BLOCK = 256
MOBA_TOPK = 3
ALIBI_MAX_BIAS = 8.0
TOPK_IN_GROUP = 2
RMS_EPS = 1e-6
NEG_BIG = -1e30
DEC_PAGES_PER_STEP = 8


def _cparams(sem, vmem=None):
    return pltpu.CompilerParams(dimension_semantics=sem, vmem_limit_bytes=vmem)


def _pick_tile(n, cap, mult):
    if n <= cap:
        return n
    best = None
    t = mult
    while t <= cap:
        if n % t == 0:
            best = t
        t += mult
    assert best is not None, (n, cap, mult)
    return best


def _split3(x):
    hi = x.astype(BF16)
    r = x - hi.astype(F32)
    mid = r.astype(BF16)
    lo = (r - mid.astype(F32)).astype(BF16)
    return hi, mid, lo


def _dot(a, b):
    return jnp.dot(a, b, preferred_element_type=F32)


def _dot_nt(a, b):
    return lax.dot_general(a, b, (((1,), (1,)), ((), ())), preferred_element_type=F32)


def _dot3(x, u):
    hi, mid, lo = _split3(x)
    return _dot(hi, u) + _dot(mid, u) + _dot(lo, u)


def _softplus(z):
    return jnp.maximum(z, 0.0) + jnp.log1p(jnp.exp(-jnp.abs(z)))


def _log_sigmoid(z):
    return jnp.minimum(z, 0.0) - jnp.log1p(jnp.exp(-jnp.abs(z)))


def _iota(shape, dim):
    return lax.broadcasted_iota(jnp.int32, shape, dim)


def _rms_kernel(x_ref, g_ref, o_ref):
    x = x_ref[...]
    ms = jnp.mean(x * x, axis=-1, keepdims=True)
    o_ref[...] = ((x * lax.rsqrt(ms + RMS_EPS)) * g_ref[...]).astype(o_ref.dtype)


def rmsnorm(x, g, out_dtype):
    m, d = x.shape
    tm = _pick_tile(m, 256, 8)
    return pl.pallas_call(
        _rms_kernel,
        out_shape=jax.ShapeDtypeStruct((m, d), out_dtype),
        grid=(m // tm,),
        in_specs=[pl.BlockSpec((tm, d), lambda i: (i, 0)),
                  pl.BlockSpec((1, d), lambda i: (0, 0))],
        out_specs=pl.BlockSpec((tm, d), lambda i: (i, 0)),
        compiler_params=_cparams(("parallel",)),
        name="rmsnorm",
    )(x, g.reshape(1, d))


def _add_rms_kernel(a_ref, b_ref, g_ref, *out_refs):
    x = a_ref[...] + b_ref[...]
    ms = jnp.mean(x * x, axis=-1, keepdims=True)
    y = (x * lax.rsqrt(ms + RMS_EPS)) * g_ref[...]
    out_refs[-1][...] = y.astype(out_refs[-1].dtype)
    if len(out_refs) == 2:
        out_refs[0][...] = x


def add_rmsnorm(a, b, g, out_dtype, write_sum):
    m, d = a.shape
    tm = _pick_tile(m, 256, 8)
    row = pl.BlockSpec((tm, d), lambda i: (i, 0))
    out_shape = [jax.ShapeDtypeStruct((m, d), out_dtype)]
    out_specs = [row]
    if write_sum:
        out_shape.insert(0, jax.ShapeDtypeStruct((m, d), F32))
        out_specs.insert(0, row)
    res = pl.pallas_call(
        _add_rms_kernel,
        out_shape=tuple(out_shape),
        grid=(m // tm,),
        in_specs=[row, row, pl.BlockSpec((1, d), lambda i: (0, 0))],
        out_specs=tuple(out_specs),
        compiler_params=_cparams(("parallel",), VMEM_LIMIT_MID),
        name="add_rmsnorm",
    )(a, b, g.reshape(1, d))
    return res if write_sum else res[0]


def _mm_kernel(x_ref, w_ref, o_ref):
    o_ref[...] = _dot(x_ref[...], w_ref[...]).astype(o_ref.dtype)


def _mm_resid_kernel(x_ref, w_ref, r_ref, o_ref):
    o_ref[...] = (r_ref[...] + _dot(x_ref[...], w_ref[...])).astype(o_ref.dtype)


def matmul(x, w, layer, out_dtype, resid=None, name="matmul"):
    m, k = x.shape
    n = w.shape[2]
    tm = _pick_tile(m, 1024, 8)
    tn = _pick_tile(n, 1024, MXU_COLS)
    in_specs = [pl.BlockSpec((tm, k), lambda i, j: (i, 0)),
                pl.BlockSpec((None, k, tn), lambda i, j: (layer, 0, j))]
    args = [x, w]
    kern = _mm_kernel
    if resid is not None:
        in_specs.append(pl.BlockSpec((tm, tn), lambda i, j: (i, j)))
        args.append(resid)
        kern = _mm_resid_kernel
    return pl.pallas_call(
        kern,
        out_shape=jax.ShapeDtypeStruct((m, n), out_dtype),
        grid=(m // tm, n // tn),
        in_specs=in_specs,
        out_specs=pl.BlockSpec((tm, tn), lambda i, j: (i, j)),
        compiler_params=_cparams(("parallel", "parallel"), VMEM_LIMIT_BIG),
        name=name,
    )(*args)


def _logf_kernel(u_ref, b_ref, lf_ref):
    lf_ref[...] = _log_sigmoid(u_ref[...] + b_ref[...])


def logf_only(u, b_f_pad, col_block):
    m = u.shape[0]
    return pl.pallas_call(
        _logf_kernel,
        out_shape=jax.ShapeDtypeStruct((m, LANES), F32),
        grid=(1,),
        in_specs=[pl.BlockSpec((m, LANES), lambda i: (0, col_block)),
                  pl.BlockSpec((1, LANES), lambda i: (0, 0))],
        out_specs=pl.BlockSpec((m, LANES), lambda i: (0, 0)),
        name="logf_decode",
    )(u, b_f_pad)


def _logf_cum_kernel(u_ref, b_ref, lf_ref, c_ref, *, chunk):
    t = u_ref.shape[0]
    lf = _log_sigmoid(u_ref[...] + b_ref[...])
    lf_ref[...] = lf
    tri = (_iota((chunk, chunk), 0) >= _iota((chunk, chunk), 1)).astype(BF16)
    carry = jnp.zeros((1, LANES), F32)
    for n in range(t // chunk):
        hi, mid, lo = _split3(lf[n * chunk:(n + 1) * chunk])
        c = _dot(tri, hi) + _dot(tri, mid) + _dot(tri, lo) + carry
        c_ref[n * chunk:(n + 1) * chunk, :] = c
        carry = c[chunk - 1:chunk, :]


def logf_cumsum(u, b_f_pad, batch, seq, col_block):
    chunk = _pick_tile(seq, 256, 8)
    kern = functools.partial(_logf_cum_kernel, chunk=chunk)
    return pl.pallas_call(
        kern,
        out_shape=(jax.ShapeDtypeStruct((batch * seq, LANES), F32),
                   jax.ShapeDtypeStruct((batch * seq, LANES), F32)),
        grid=(batch,),
        in_specs=[pl.BlockSpec((seq, LANES), lambda b: (b, col_block)),
                  pl.BlockSpec((1, LANES), lambda b: (0, 0))],
        out_specs=(pl.BlockSpec((seq, LANES), lambda b: (b, 0)),
                   pl.BlockSpec((seq, LANES), lambda b: (b, 0))),
        compiler_params=_cparams(("parallel",)),
        name="logf_cumsum",
    )(u, b_f_pad)


def _stack_heads(q, g_heads):
    return jnp.concatenate([q[:, g * HEAD_DIM:(g + 1) * HEAD_DIM] for g in range(g_heads)], axis=0)


def _unstack_heads(o, g_heads, tq):
    return jnp.concatenate([o[g * tq:(g + 1) * tq] for g in range(g_heads)], axis=1)


def _tile_pos(g_heads, tq):
    row_t = jnp.concatenate([_iota((tq, tq), 0)] * g_heads, axis=0)
    col = _iota((g_heads * tq, tq), 1)
    return row_t, col


def _attn_a_kernel(q_ref, k_ref, v_ref, o_ref, *, tq, g_heads, scale):
    i = pl.program_id(2)
    rows = g_heads * tq
    qs = _stack_heads(q_ref[...], g_heads).astype(BF16)
    row_t, col = _tile_pos(g_heads, tq)
    suffix = (_iota((tq, tq), 0) >= _iota((tq, tq), 1)).astype(BF16)

    def block(j, carry, masked):
        o, c = carry
        start = pl.multiple_of(j * tq, tq)
        k = k_ref[pl.ds(start, tq), :].astype(BF16)
        v = v_ref[pl.ds(start, tq), :].astype(BF16)
        z = _dot_nt(qs, k) * scale
        sp = _softplus(z)
        if masked:
            reads = col < row_t
            sp = jnp.where(reads, sp, 0.0)
        hi = sp.astype(BF16)
        lo = (sp - hi.astype(F32)).astype(BF16)
        s_incl = c + _dot(hi, suffix) + _dot(lo, suffix)
        w = jnp.exp(z - s_incl)
        if masked:
            w = jnp.where(reads, w, 0.0)
        o = o + _dot(w.astype(BF16), v)
        return o, s_incl[:, 0:1]

    carry = (jnp.zeros((rows, HEAD_DIM), F32), jnp.zeros((rows, 1), F32))
    carry = block(i, carry, True)
    carry = lax.fori_loop(0, i, lambda s, cr: block(i - 1 - s, cr, False), carry)
    o_ref[...] = _unstack_heads(carry[0], g_heads, tq).astype(o_ref.dtype)


def _attn_b_kernel(q_ref, k_ref, v_ref, c_ref, o_ref, *, tq, g_heads, scale):
    i = pl.program_id(2)
    rows = g_heads * tq
    qs = _stack_heads(q_ref[...], g_heads).astype(BF16)
    row_t, col = _tile_pos(g_heads, tq)
    eye = _iota((tq, tq), 0) == _iota((tq, tq), 1)
    cq = jnp.concatenate(
        [jnp.sum(jnp.where(eye, c_ref[g, i], 0.0), axis=1, keepdims=True) for g in range(g_heads)],
        axis=0)

    def block(j, carry, masked):
        m, l, acc = carry
        start = pl.multiple_of(j * tq, tq)
        k = k_ref[pl.ds(start, tq), :].astype(BF16)
        v = v_ref[pl.ds(start, tq), :].astype(BF16)
        z = _dot_nt(qs, k) * scale
        ck = jnp.concatenate([jnp.broadcast_to(c_ref[g, j], (tq, tq)) for g in range(g_heads)], axis=0)
        z = (z + cq) - ck
        if masked:
            z = jnp.where(col <= row_t, z, NEG_BIG)
        m_new = jnp.maximum(m, jnp.max(z, axis=1, keepdims=True))
        alpha = jnp.exp(m - m_new)
        p = jnp.exp(z - m_new)
        l = alpha * l + jnp.sum(p, axis=1, keepdims=True)
        acc = alpha * acc + _dot(p.astype(BF16), v)
        return m_new, l, acc

    carry = (jnp.full((rows, 1), NEG_BIG, F32), jnp.zeros((rows, 1), F32),
             jnp.zeros((rows, HEAD_DIM), F32))
    carry = block(i, carry, True)
    carry = lax.fori_loop(0, i, lambda s, cr: block(s, cr, False), carry)
    o = carry[2] / carry[1]
    o_ref[...] = _unstack_heads(o, g_heads, tq).astype(o_ref.dtype)


def _topk_rank_lanes(gate, lane, n_valid_lanes):
    rank = jnp.zeros(gate.shape, F32)
    for n in range(n_valid_lanes):
        gn = gate[:, n:n + 1]
        beats = (gate > gn) | ((gate == gn) & (lane < n))
        cnt = jnp.sum(beats.astype(F32), axis=1, keepdims=True)
        rank = jnp.where(lane == n, cnt, rank)
    return rank


def _attn_c_kernel(slopes_ref, q_ref, k_ref, v_ref, o_ref, km_sc, m_sc, l_sc, acc_sc,
                   *, tq, g_heads, scale, nb):
    h = pl.program_id(1)
    i = pl.program_id(2)
    rows = g_heads * tq

    @pl.when(i == 0)
    def _():
        km_sc[...] = jnp.zeros(km_sc.shape, F32)
        for n in range(nb):
            km_sc[n:n + 1, :] = jnp.sum(k_ref[n * tq:(n + 1) * tq, :], axis=0, keepdims=True) * (1.0 / tq)

    qs = _stack_heads(q_ref[...], g_heads).astype(BF16)
    row_t, col = _tile_pos(g_heads, tq)
    slope = jnp.concatenate(
        [jnp.full((tq, 1), slopes_ref[h * g_heads + g], F32) for g in range(g_heads)], axis=0)

    lane = _iota((rows, LANES), 1)
    gate = _dot_nt(qs, km_sc[...].astype(BF16))
    gate = jnp.where(lane < i, gate, -jnp.inf)
    rank = _topk_rank_lanes(gate, lane, nb)
    sel = jnp.where((rank < float(MOBA_TOPK)) & (lane < i), 1.0, 0.0)

    def scores(n_start, k):
        z = _dot_nt(qs, k) * scale
        dist = ((i * tq - n_start) + (row_t - col)).astype(F32)
        return z - slope * dist

    start = pl.multiple_of(i * tq, tq)
    k = k_ref[pl.ds(start, tq), :].astype(BF16)
    v = v_ref[pl.ds(start, tq), :].astype(BF16)
    z = jnp.where(col <= row_t, scores(i * tq, k), NEG_BIG)
    m0 = jnp.max(z, axis=1, keepdims=True)
    p = jnp.exp(z - m0)
    m_sc[...] = m0
    l_sc[...] = jnp.sum(p, axis=1, keepdims=True)
    acc_sc[...] = _dot(p.astype(BF16), v)

    for n in range(nb):
        @pl.when(n < i)
        def _(n=n):
            k = k_ref[n * tq:(n + 1) * tq, :].astype(BF16)
            v = v_ref[n * tq:(n + 1) * tq, :].astype(BF16)
            take = sel[:, n:n + 1] > 0.0
            z = jnp.where(take, scores(n * tq, k), NEG_BIG)
            m_old = m_sc[...]
            m_new = jnp.maximum(m_old, jnp.max(z, axis=1, keepdims=True))
            alpha = jnp.exp(m_old - m_new)
            p = jnp.where(take, jnp.exp(z - m_new), 0.0)
            m_sc[...] = m_new
            l_sc[...] = alpha * l_sc[...] + jnp.sum(p, axis=1, keepdims=True)
            acc_sc[...] = alpha * acc_sc[...] + _dot(p.astype(BF16), v)

    o = acc_sc[...] / l_sc[...]
    o_ref[...] = _unstack_heads(o, g_heads, tq).astype(o_ref.dtype)


def prompt_attention(kind, u, batch, seq, n_q_heads, n_kv, q_off, kv_off, extra=None):
    g_heads = n_q_heads // n_kv
    tq = Q_TILE
    assert seq % tq == 0 and q_off % (g_heads * HEAD_DIM) == 0 and kv_off % HEAD_DIM == 0
    nq = seq // tq
    qw = g_heads * HEAD_DIM
    scale = HEAD_DIM ** -0.5
    q_spec = pl.BlockSpec((tq, qw), lambda b, h, i, *_: (b * nq + i, q_off // qw + h))
    k_spec = pl.BlockSpec((seq, HEAD_DIM), lambda b, h, i, *_: (b, kv_off // HEAD_DIM + h))
    v_spec = pl.BlockSpec((seq, HEAD_DIM), lambda b, h, i, *_: (b, kv_off // HEAD_DIM + n_kv + h))
    o_spec = pl.BlockSpec((tq, qw), lambda b, h, i, *_: (b * nq + i, h))
    out_shape = jax.ShapeDtypeStruct((batch * seq, n_q_heads * HEAD_DIM), BF16)
    grid = (batch, n_kv, nq)
    sem = ("parallel", "parallel", "arbitrary")
    if kind == "a":
        kern = functools.partial(_attn_a_kernel, tq=tq, g_heads=g_heads, scale=scale)
        return pl.pallas_call(
            kern, out_shape=out_shape, grid=grid,
            in_specs=[q_spec, k_spec, v_spec], out_specs=o_spec,
            compiler_params=_cparams(sem, VMEM_LIMIT_MID), name="attn_stickbreak",
        )(u, u, u)
    if kind == "b":
        c_rows = extra
        c_spec = pl.BlockSpec((None, g_heads, nq, 1, tq), lambda b, h, i: (b, h, 0, 0, 0))
        kern = functools.partial(_attn_b_kernel, tq=tq, g_heads=g_heads, scale=scale)
        return pl.pallas_call(
            kern, out_shape=out_shape, grid=grid,
            in_specs=[q_spec, k_spec, v_spec, c_spec], out_specs=o_spec,
            compiler_params=_cparams(sem, VMEM_LIMIT_MID), name="attn_forget",
        )(u, u, u, c_rows)
    slopes = extra
    nb = seq // MOBA_BLOCK
    assert tq == MOBA_BLOCK and nb <= LANES
    rows = g_heads * tq
    kern = functools.partial(_attn_c_kernel, tq=tq, g_heads=g_heads, scale=scale, nb=nb)
    return pl.pallas_call(
        kern, out_shape=out_shape,
        grid_spec=pltpu.PrefetchScalarGridSpec(
            num_scalar_prefetch=1, grid=grid,
            in_specs=[q_spec, k_spec, v_spec], out_specs=o_spec,
            scratch_shapes=[pltpu.VMEM((LANES, HEAD_DIM), F32), pltpu.VMEM((rows, 1), F32),
                            pltpu.VMEM((rows, 1), F32), pltpu.VMEM((rows, HEAD_DIM), F32)]),
        compiler_params=_cparams(sem, VMEM_LIMIT_MID), name="attn_moba",
    )(slopes, u, u, u)


def _diag_extract(acc, n_kv, g_heads):
    row = _iota((acc.shape[0], HEAD_DIM), 0)
    out = jnp.zeros((acc.shape[0], HEAD_DIM), F32)
    for kvh in range(n_kv):
        mine = (row >= kvh * g_heads) & (row < (kvh + 1) * g_heads)
        out = out + jnp.where(mine, acc[:, kvh * HEAD_DIM:(kvh + 1) * HEAD_DIM], 0.0)
    return out


def _dec_a_kernel(pt_ref, q_ref, *rest, n_pages_step, n_kv, g_heads, scale):
    page_refs = rest[:n_pages_step]
    o_ref, acc_sc, c_sc = rest[n_pages_step:]
    j = pl.program_id(1)
    kvw = n_kv * HEAD_DIM
    page = page_refs[0].shape[0]

    @pl.when(j == 0)
    def _():
        acc_sc[...] = jnp.zeros(acc_sc.shape, F32)
        c_sc[...] = jnp.zeros(c_sc.shape, F32)

    qb = q_ref[...]
    suffix = (_iota((page, page), 0) >= _iota((page, page), 1)).astype(BF16)
    acc = acc_sc[...]
    c = c_sc[...]
    for r in range(n_pages_step):
        kp = page_refs[r][:, :kvw].astype(BF16)
        vp = page_refs[r][:, kvw:].astype(BF16)
        z = _dot_nt(qb, kp) * scale
        sp = _softplus(z)
        hi = sp.astype(BF16)
        lo = (sp - hi.astype(F32)).astype(BF16)
        s_incl = c + _dot(hi, suffix) + _dot(lo, suffix)
        w = jnp.exp(z - s_incl)
        acc = acc + _dot(w.astype(BF16), vp)
        c = s_incl[:, 0:1]
    acc_sc[...] = acc
    c_sc[...] = c

    @pl.when(j == pl.num_programs(1) - 1)
    def _():
        o_ref[...] = _diag_extract(acc, n_kv, g_heads)


def _dec_b_kernel(pt_ref, q_ref, knew_ref, vnew_ref, lfnew_ref, *rest, n_pages_step, n_kv, g_heads, scale):
    page_refs = rest[:n_pages_step]
    lf_refs = rest[n_pages_step:2 * n_pages_step]
    o_ref, m_sc, l_sc, acc_sc, cb_sc = rest[2 * n_pages_step:]
    j = pl.program_id(1)
    kvw = n_kv * HEAD_DIM
    page = page_refs[0].shape[0]
    qb = q_ref[...]

    @pl.when(j == 0)
    def _():
        knew = knew_ref[...].astype(BF16).astype(F32)
        z_new = jnp.sum(qb.astype(F32) * knew, axis=1, keepdims=True) * scale
        m_sc[...] = z_new
        l_sc[...] = jnp.ones(l_sc.shape, F32)
        acc_sc[...] = jnp.broadcast_to(vnew_ref[...].astype(BF16).astype(F32), acc_sc.shape)
        cb_sc[...] = lfnew_ref[:, 0:1]

    after = (_iota((page, page), 0) > _iota((page, page), 1)).astype(BF16)
    m = m_sc[...]
    l = l_sc[...]
    acc = acc_sc[...]
    cb = cb_sc[...]
    for r in range(n_pages_step):
        kp = page_refs[r][:, :kvw].astype(BF16)
        vp = page_refs[r][:, kvw:].astype(BF16)
        lf = lf_refs[r][...]
        later = _dot3(lf, after)
        z = _dot_nt(qb, kp) * scale + (cb + later)
        m_new = jnp.maximum(m, jnp.max(z, axis=1, keepdims=True))
        alpha = jnp.exp(m - m_new)
        p = jnp.exp(z - m_new)
        l = alpha * l + jnp.sum(p, axis=1, keepdims=True)
        acc = alpha * acc + _dot(p.astype(BF16), vp)
        m = m_new
        cb = cb + (later[:, 0:1] + lf[:, 0:1])
    m_sc[...] = m
    l_sc[...] = l
    acc_sc[...] = acc
    cb_sc[...] = cb

    @pl.when(j == pl.num_programs(1) - 1)
    def _():
        o_ref[...] = _diag_extract(acc, n_kv, g_heads) / l


def _dec_c_gate_kernel(pt_ref, q_ref, *rest, n_pages_blk, n_kv):
    page_refs = rest[:n_pages_blk]
    o_ref = rest[n_pages_blk]
    tot = jnp.zeros((1, n_kv * HEAD_DIM), F32)
    n_keys = 0
    for r in range(n_pages_blk):
        tot = tot + jnp.sum(page_refs[r][...], axis=0, keepdims=True)
        n_keys += page_refs[r].shape[0]
    km = (tot * (1.0 / n_keys)).astype(BF16).astype(F32)
    g = jnp.sum(q_ref[...].astype(F32) * km, axis=1, keepdims=True)
    o_ref[...] = jnp.broadcast_to(g, o_ref.shape)


def _dec_c_kernel(pt_ref, q_ref, knew_ref, vnew_ref, slope_ref, gate_ref, *rest,
                  n_pages_blk, n_kv, g_heads, scale, past_len, n_blocks):
    page_refs = rest[:n_pages_blk]
    o_ref, sel_sc, m_sc, l_sc, acc_sc = rest[n_pages_blk:]
    j = pl.program_id(1)
    kvw = n_kv * HEAD_DIM
    page = page_refs[0].shape[0]
    qb = q_ref[...]
    rows = qb.shape[0]
    lane = _iota((rows, LANES), 1)

    @pl.when(j == 0)
    def _():
        gate = gate_ref[...]
        rank = _topk_rank_lanes(gate, lane, n_blocks)
        sel_sc[...] = jnp.where((rank < float(MOBA_TOPK)) & (lane < n_blocks), 1.0, 0.0)
        knew = knew_ref[...].astype(BF16).astype(F32)
        m_sc[...] = jnp.sum(qb.astype(F32) * knew, axis=1, keepdims=True) * scale
        l_sc[...] = jnp.ones(l_sc.shape, F32)
        acc_sc[...] = jnp.broadcast_to(vnew_ref[...].astype(BF16).astype(F32), acc_sc.shape)

    take = jnp.sum(jnp.where(lane == j, sel_sc[...], 0.0), axis=1, keepdims=True) > 0.0
    slope = slope_ref[:, 0:1]
    m = m_sc[...]
    l = l_sc[...]
    acc = acc_sc[...]
    for r in range(n_pages_blk):
        kp = page_refs[r][:, :kvw].astype(BF16)
        vp = page_refs[r][:, kvw:].astype(BF16)
        pos = j * (n_pages_blk * page) + r * page + _iota((rows, page), 1)
        dist = (past_len - pos).astype(F32)
        z = jnp.where(take, _dot_nt(qb, kp) * scale - slope * dist, NEG_BIG)
        m_new = jnp.maximum(m, jnp.max(z, axis=1, keepdims=True))
        alpha = jnp.exp(m - m_new)
        p = jnp.where(take, jnp.exp(z - m_new), 0.0)
        l = alpha * l + jnp.sum(p, axis=1, keepdims=True)
        acc = alpha * acc + _dot(p.astype(BF16), vp)
        m = m_new
    m_sc[...] = m
    l_sc[...] = l
    acc_sc[...] = acc

    @pl.when(j == pl.num_programs(1) - 1)
    def _():
        o_ref[...] = _diag_extract(acc, n_kv, g_heads) / l


def _blockdiag_q(q, n_kv, rows):
    s, h, _ = q.shape
    g_heads = h // n_kv
    onehot = (jnp.arange(h)[:, None] // g_heads == jnp.arange(n_kv)[None, :]).astype(q.dtype)
    qb = (q[:, :, None, :] * onehot[None, :, :, None]).reshape(s, h, n_kv * HEAD_DIM)
    qb = jnp.pad(qb, ((0, 0), (0, rows - h), (0, 0)))
    return qb.astype(BF16)


def _pad_rows(x, rows):
    return jnp.pad(x, ((0, 0), (0, rows - x.shape[1])) + ((0, 0),) * (x.ndim - 2))


def decode_attention_a(q, cache, layer, page_table, n_kv):
    s, h, _ = q.shape
    rows = -(-h // 8) * 8
    g_heads = h // n_kv
    n_pages = page_table.shape[1]
    page, roww = cache.shape[2], cache.shape[3]
    pps = _pick_tile(n_pages, DEC_PAGES_PER_STEP, 1)
    steps = n_pages // pps
    qb = _blockdiag_q(q, n_kv, rows)

    def page_spec(r):
        return pl.BlockSpec((None, None, page, roww),
                            lambda b, j, pt: (layer, pt[b, n_pages - 1 - (j * pps + r)], 0, 0))

    kern = functools.partial(_dec_a_kernel, n_pages_step=pps, n_kv=n_kv, g_heads=g_heads,
                             scale=HEAD_DIM ** -0.5)
    out = pl.pallas_call(
        kern,
        out_shape=jax.ShapeDtypeStruct((s, rows, HEAD_DIM), F32),
        grid_spec=pltpu.PrefetchScalarGridSpec(
            num_scalar_prefetch=1, grid=(s, steps),
            in_specs=[pl.BlockSpec((None, rows, n_kv * HEAD_DIM), lambda b, j, pt: (b, 0, 0))]
            + [page_spec(r) for r in range(pps)],
            out_specs=pl.BlockSpec((None, rows, HEAD_DIM), lambda b, j, pt: (b, 0, 0)),
            scratch_shapes=[pltpu.VMEM((rows, n_kv * HEAD_DIM), F32), pltpu.VMEM((rows, 1), F32)]),
        compiler_params=_cparams(("parallel", "arbitrary"), VMEM_LIMIT_MID),
        name="decode_stickbreak",
    )(page_table, qb, *([cache] * pps))
    return out[:, :h].reshape(s, h * HEAD_DIM)


def decode_attention_b(q, k_new, v_new, lf_new, cache, lf_cache_t, layer, page_table, n_kv):
    s, h, _ = q.shape
    rows = lf_cache_t.shape[2]
    g_heads = h // n_kv
    n_pages = page_table.shape[1]
    page, roww = cache.shape[2], cache.shape[3]
    kvw = n_kv * HEAD_DIM
    pps = _pick_tile(n_pages, DEC_PAGES_PER_STEP, 1)
    steps = n_pages // pps
    qb = _blockdiag_q(q, n_kv, rows)
    lfn = jnp.broadcast_to(_pad_rows(lf_new, rows)[:, :, None], (s, rows, LANES))

    def page_spec(r):
        return pl.BlockSpec((None, None, page, roww),
                            lambda b, j, pt: (layer, pt[b, n_pages - 1 - (j * pps + r)], 0, 0))

    def lf_spec(r):
        return pl.BlockSpec((None, None, rows, page),
                            lambda b, j, pt: (layer, pt[b, n_pages - 1 - (j * pps + r)], 0, 0))

    per_seq = lambda b, j, pt: (b, 0, 0)
    kern = functools.partial(_dec_b_kernel, n_pages_step=pps, n_kv=n_kv, g_heads=g_heads,
                             scale=HEAD_DIM ** -0.5)
    out = pl.pallas_call(
        kern,
        out_shape=jax.ShapeDtypeStruct((s, rows, HEAD_DIM), F32),
        grid_spec=pltpu.PrefetchScalarGridSpec(
            num_scalar_prefetch=1, grid=(s, steps),
            in_specs=[pl.BlockSpec((None, rows, kvw), per_seq),
                      pl.BlockSpec((None, 1, kvw), per_seq),
                      pl.BlockSpec((None, 1, kvw), per_seq),
                      pl.BlockSpec((None, rows, LANES), per_seq)]
            + [page_spec(r) for r in range(pps)] + [lf_spec(r) for r in range(pps)],
            out_specs=pl.BlockSpec((None, rows, HEAD_DIM), per_seq),
            scratch_shapes=[pltpu.VMEM((rows, 1), F32), pltpu.VMEM((rows, 1), F32),
                            pltpu.VMEM((rows, kvw), F32), pltpu.VMEM((rows, 1), F32)]),
        compiler_params=_cparams(("parallel", "arbitrary"), VMEM_LIMIT_MID),
        name="decode_forget",
    )(page_table, qb, k_new[:, None, :], v_new[:, None, :], lfn, *([cache] * pps), *([lf_cache_t] * pps))
    return out[:, :h].reshape(s, h * HEAD_DIM)


def decode_attention_c(q, k_new, v_new, slopes, cache, layer, page_table, n_kv):
    s, h, _ = q.shape
    rows = -(-h // 8) * 8
    g_heads = h // n_kv
    n_pages = page_table.shape[1]
    page, roww = cache.shape[2], cache.shape[3]
    kvw = n_kv * HEAD_DIM
    assert MOBA_BLOCK % page == 0
    ppb = MOBA_BLOCK // page
    assert n_pages % ppb == 0
    n_blocks = n_pages // ppb
    assert n_blocks <= LANES
    past_len = n_pages * page
    qb = _blockdiag_q(q, n_kv, rows)
    per_seq = lambda b, j, pt: (b, 0, 0)

    def k_spec(r):
        return pl.BlockSpec((None, None, page, kvw), lambda b, j, pt: (layer, pt[b, j * ppb + r], 0, 0))

    gates = pl.pallas_call(
        functools.partial(_dec_c_gate_kernel, n_pages_blk=ppb, n_kv=n_kv),
        out_shape=jax.ShapeDtypeStruct((s, n_blocks, rows, LANES), F32),
        grid_spec=pltpu.PrefetchScalarGridSpec(
            num_scalar_prefetch=1, grid=(s, n_blocks),
            in_specs=[pl.BlockSpec((None, rows, kvw), per_seq)] + [k_spec(r) for r in range(ppb)],
            out_specs=pl.BlockSpec((None, None, rows, LANES), lambda b, j, pt: (b, j, 0, 0))),
        compiler_params=_cparams(("parallel", "arbitrary")),
        name="decode_moba_gate",
    )(page_table, qb, *([cache] * ppb))
    gate_t = jnp.transpose(gates[:, :, :, 0], (0, 2, 1))
    gate_t = jnp.pad(gate_t, ((0, 0), (0, 0), (0, LANES - n_blocks)), constant_values=-jnp.inf)
    slope_b = jnp.broadcast_to(jnp.pad(slopes, (0, rows - h))[:, None], (rows, LANES))

    def page_spec(r):
        return pl.BlockSpec((None, None, page, roww), lambda b, j, pt: (layer, pt[b, j * ppb + r], 0, 0))

    kern = functools.partial(_dec_c_kernel, n_pages_blk=ppb, n_kv=n_kv, g_heads=g_heads,
                             scale=HEAD_DIM ** -0.5, past_len=past_len, n_blocks=n_blocks)
    out = pl.pallas_call(
        kern,
        out_shape=jax.ShapeDtypeStruct((s, rows, HEAD_DIM), F32),
        grid_spec=pltpu.PrefetchScalarGridSpec(
            num_scalar_prefetch=1, grid=(s, n_blocks),
            in_specs=[pl.BlockSpec((None, rows, kvw), per_seq),
                      pl.BlockSpec((None, 1, kvw), per_seq),
                      pl.BlockSpec((None, 1, kvw), per_seq),
                      pl.BlockSpec((rows, LANES), lambda b, j, pt: (0, 0)),
                      pl.BlockSpec((None, rows, LANES), per_seq)]
            + [page_spec(r) for r in range(ppb)],
            out_specs=pl.BlockSpec((None, rows, HEAD_DIM), per_seq),
            scratch_shapes=[pltpu.VMEM((rows, LANES), F32), pltpu.VMEM((rows, 1), F32),
                            pltpu.VMEM((rows, 1), F32), pltpu.VMEM((rows, kvw), F32)]),
        compiler_params=_cparams(("parallel", "arbitrary"), VMEM_LIMIT_MID),
        name="decode_moba",
    )(page_table, qb, k_new[:, None, :], v_new[:, None, :], slope_b, gate_t, *([cache] * ppb))
    return out[:, :h].reshape(s, h * HEAD_DIM)


def _merge_kernel(xn_ref, oa_ref, ob_ref, oc_ref, wg_ref, wa_ref, wb_ref, wc_ref, o_ref):
    xn = xn_ref[...]
    acc = None
    for br, (o_r, w_r) in enumerate(((oa_ref, wa_ref), (ob_ref, wb_ref), (oc_ref, wc_ref))):
        gate = jax.nn.sigmoid(_dot(xn, wg_ref[br]))
        term = gate * _dot(o_r[...], w_r[...])
        acc = term if acc is None else acc + term
    o_ref[...] = acc.astype(o_ref.dtype)


def branch_merge(xn, o_a, o_b, o_c, wg, w_a, w_b, w_c, layer):
    m, d = xn.shape
    tm = _pick_tile(m, 512, 8)
    tn = _pick_tile(d, 256, LANES)
    act = lambda a: pl.BlockSpec((tm, a.shape[1]), lambda i, j: (i, 0))
    wsp = lambda w: pl.BlockSpec((None, w.shape[1], tn), lambda i, j: (layer, 0, j))
    return pl.pallas_call(
        _merge_kernel,
        out_shape=jax.ShapeDtypeStruct((m, d), BF16),
        grid=(m // tm, d // tn),
        in_specs=[act(xn), act(o_a), act(o_b), act(o_c),
                  pl.BlockSpec((None, 3, d, tn), lambda i, j: (layer, 0, 0, j)),
                  wsp(w_a), wsp(w_b), wsp(w_c)],
        out_specs=pl.BlockSpec((tm, tn), lambda i, j: (i, j)),
        compiler_params=_cparams(("parallel", "parallel"), VMEM_LIMIT_BIG),
        name="branch_merge",
    )(xn, o_a, o_b, o_c, wg, w_a, w_b, w_c)


def _router_kernel(x_ref, g_ref, w_ref, b_ref, xn_ref, comb_ref, *, n_groups, e_per_group):
    x = x_ref[...]
    ms = jnp.mean(x * x, axis=-1, keepdims=True)
    xn = (x * lax.rsqrt(ms + RMS_EPS)) * g_ref[...]
    xn_ref[...] = xn.astype(xn_ref.dtype)
    xh, xm, xl = _split3(xn)
    wh, wm, wl = w_ref[0], w_ref[1], w_ref[2]
    logits = (_dot(xh, wh) + (_dot(xh, wm) + _dot(xm, wh))
              + (_dot(xh, wl) + _dot(xm, wm) + _dot(xl, wh))) + b_ref[...]
    rows = logits.shape[0]
    lane = _iota((rows, LANES), 1).astype(F32)
    n_experts = n_groups * e_per_group
    is_g = lane < n_groups
    lg = jnp.where(is_g, logits, -jnp.inf)
    mg = jnp.max(lg, axis=1, keepdims=True)
    grp = jnp.min(jnp.where(is_g & (lg == mg), lane, float(LANES)), axis=1, keepdims=True)
    p_grp = 1.0 / jnp.sum(jnp.exp(lg - mg), axis=1, keepdims=True)
    e_lo = n_groups + grp * e_per_group
    in_grp = (lane >= e_lo) & (lane < e_lo + e_per_group)
    le = jnp.where(in_grp, logits, -jnp.inf)
    me = jnp.max(le, axis=1, keepdims=True)
    ee = jnp.exp(le - me)
    pe = ee / jnp.sum(ee, axis=1, keepdims=True)
    rank = jnp.zeros((rows, LANES), F32)
    for e in range(n_experts):
        ln = n_groups + e
        pv = pe[:, ln:ln + 1]
        beats = in_grp & ((pe > pv) | ((pe == pv) & (lane < ln)))
        cnt = jnp.sum(beats.astype(F32), axis=1, keepdims=True)
        rank = jnp.where(lane == ln, cnt, rank)
    top = in_grp & (rank < float(TOPK_IN_GROUP))
    top_p = jnp.where(top, pe, 0.0)
    top_p = top_p / jnp.sum(top_p, axis=1, keepdims=True)
    comb_ref[...] = p_grp * top_p


def moe_router(h, g, w_r3, b_r, layer, n_groups, e_per_group):
    m, d = h.shape
    tm = _pick_tile(m, 256, 8)
    kern = functools.partial(_router_kernel, n_groups=n_groups, e_per_group=e_per_group)
    return pl.pallas_call(
        kern,
        out_shape=(jax.ShapeDtypeStruct((m, d), BF16), jax.ShapeDtypeStruct((m, LANES), F32)),
        grid=(m // tm,),
        in_specs=[pl.BlockSpec((tm, d), lambda i: (i, 0)),
                  pl.BlockSpec((1, d), lambda i: (0, 0)),
                  pl.BlockSpec((None, 3, d, LANES), lambda i: (layer, 0, 0, 0)),
                  pl.BlockSpec((None, 1, LANES), lambda i: (layer, 0, 0))],
        out_specs=(pl.BlockSpec((tm, d), lambda i: (i, 0)),
                   pl.BlockSpec((tm, LANES), lambda i: (i, 0))),
        compiler_params=_cparams(("parallel",), VMEM_LIMIT_MID),
        name="moe_router",
    )(h, g.reshape(1, d), w_r3, b_r)


def _experts_kernel(x_ref, comb_ref, wg_ref, wu_ref, wd_ref, o_ref, *, n_groups):
    e = pl.program_id(1)
    f = pl.program_id(2)

    @pl.when((e == 0) & (f == 0))
    def _():
        o_ref[...] = jnp.zeros(o_ref.shape, F32)

    x = x_ref[...]
    lane = _iota(comb_ref.shape, 1)
    w_tok = jnp.sum(jnp.where(lane == n_groups + e, comb_ref[...], 0.0), axis=1, keepdims=True)
    hg = _dot(x, wg_ref[...])
    hu = _dot(x, wu_ref[...])
    hid = (hg * jax.nn.sigmoid(hg)) * hu * w_tok
    o_ref[...] += _dot(hid.astype(BF16), wd_ref[...])


def moe_experts(xn, comb, w_eg, w_eu, w_ed, layer, n_groups):
    m, d = xn.shape
    n_experts, d_ff = w_eg.shape[1], w_eg.shape[3]
    tm = _pick_tile(m, 512, 8)
    tf = _pick_tile(d_ff, 512, LANES)
    kern = functools.partial(_experts_kernel, n_groups=n_groups)
    return pl.pallas_call(
        kern,
        out_shape=jax.ShapeDtypeStruct((m, d), F32),
        grid=(m // tm, n_experts, d_ff // tf),
        in_specs=[pl.BlockSpec((tm, d), lambda i, e, f: (i, 0)),
                  pl.BlockSpec((tm, LANES), lambda i, e, f: (i, 0)),
                  pl.BlockSpec((None, None, d, tf), lambda i, e, f: (layer, e, 0, f)),
                  pl.BlockSpec((None, None, d, tf), lambda i, e, f: (layer, e, 0, f)),
                  pl.BlockSpec((None, None, tf, d), lambda i, e, f: (layer, e, f, 0))],
        out_specs=pl.BlockSpec((tm, d), lambda i, e, f: (i, 0)),
        compiler_params=_cparams(("parallel", "arbitrary", "arbitrary"), VMEM_LIMIT_BIG),
        name="moe_experts",
    )(xn, comb, w_eg, w_eu, w_ed)


def kernel(x_prompt, x_sample, cache_kv_a, cache_kv_b, cache_logf_b, cache_kv_c, page_table, w_in, b_f, w_br_a, w_br_b, w_br_c, w_o, g_norm_mix, g_norm_ffn, w_router_group, b_router_group, w_router_expert, b_router_expert, w_exp_gate, w_exp_up, w_exp_down, g_norm_final):
    batch, seq, d_model = x_prompt.shape
    dec_batch, dec_seq, _ = x_sample.shape
    assert dec_seq == 1
    depth = w_in.shape[0]
    kv_a, kv_b, kv_c = cache_kv_a.shape[4], cache_kv_b.shape[4], cache_kv_c.shape[4]
    h_a, h_b, h_c = (w.shape[1] // HEAD_DIM for w in (w_br_a, w_br_b, w_br_c))
    assert cache_logf_b.shape[3] == h_b
    n_groups = w_router_group.shape[2]
    n_experts = w_router_expert.shape[2]
    e_per_group = n_experts // n_groups
    assert n_groups + n_experts <= LANES

    qa_w, kva_w = h_a * HEAD_DIM, 2 * kv_a * HEAD_DIM
    qb_w, kvb_w = h_b * HEAD_DIM, 2 * kv_b * HEAD_DIM
    qc_w, kvc_w = h_c * HEAD_DIM, 2 * kv_c * HEAD_DIM
    f_pad = MXU_COLS
    o_qa = 0
    o_kva = o_qa + qa_w
    o_qb = o_kva + kva_w
    o_kvb = o_qb + qb_w
    o_f = o_kvb + kvb_w
    o_qc = o_f + f_pad
    o_kvc = o_qc + qc_w
    n_u = o_kvc + kvc_w
    src_f = o_f
    src_qc = src_f + h_b
    src_g = src_qc + qc_w + kvc_w
    assert w_in.shape[2] == src_g + 3 * d_model

    w_u = jnp.concatenate(
        [w_in[:, :, :src_qc], jnp.zeros((depth, d_model, f_pad - h_b), w_in.dtype), w_in[:, :, src_qc:src_g]],
        axis=2).astype(BF16)
    w_g = jnp.transpose(w_in[:, :, src_g:].reshape(depth, d_model, 3, d_model), (0, 2, 1, 3)).astype(BF16)
    w_a16, w_b16, w_c16, w_o16 = (w.astype(BF16) for w in (w_br_a, w_br_b, w_br_c, w_o))
    w_eg16, w_eu16, w_ed16 = (w.astype(BF16) for w in (w_exp_gate, w_exp_up, w_exp_down))
    w_r = jnp.concatenate([w_router_group, w_router_expert], axis=2)
    w_r = jnp.pad(w_r, ((0, 0), (0, 0), (0, LANES - w_r.shape[2])))
    r_hi = w_r.astype(BF16)
    r_res = w_r - r_hi.astype(F32)
    r_mid = r_res.astype(BF16)
    r_lo = (r_res - r_mid.astype(F32)).astype(BF16)
    w_r3 = jnp.stack([r_hi, r_mid, r_lo], axis=1)
    b_r = jnp.concatenate([b_router_group, b_router_expert], axis=1)
    b_r = jnp.pad(b_r, ((0, 0), (0, LANES - b_r.shape[1])))[:, None, :]
    b_f_pad = jnp.pad(b_f, ((0, 0), (0, LANES - h_b)))
    slopes = 2.0 ** (-ALIBI_MAX_BIAS * jnp.arange(1, h_c + 1, dtype=F32) / h_c)

    rows_b = -(-h_b // 8) * 8
    n_pool, page = cache_kv_a.shape[1], cache_kv_a.shape[2]
    cache_a = cache_kv_a.reshape(depth, n_pool, page, kva_w)
    cache_b = cache_kv_b.reshape(depth, n_pool, page, kvb_w)
    cache_c = cache_kv_c.reshape(depth, n_pool, page, kvc_w)
    lf_cache_t = jnp.pad(jnp.transpose(cache_logf_b, (0, 1, 3, 2)), ((0, 0), (0, 0), (0, rows_b - h_b), (0, 0)))

    n_p = batch * seq
    nq = seq // Q_TILE
    xp = x_prompt.reshape(n_p, d_model)
    xs = x_sample.reshape(dec_batch, d_model)
    rows_p = {k: [] for k in ("kv_a", "kv_b", "logf", "kv_c")}
    rows_s = {k: [] for k in ("kv_a", "kv_b", "logf", "kv_c")}

    yp = ys = None
    for l in range(depth):
        if yp is None:
            xn = rmsnorm(xp, g_norm_mix[l], BF16)
        else:
            xp, xn = add_rmsnorm(xp, yp, g_norm_mix[l], BF16, write_sum=True)
        u = matmul(xn, w_u, l, F32, name="in_proj")
        lf, cum = logf_cumsum(u, b_f_pad[l][None, :], batch, seq, o_f // LANES)
        c_rows = jnp.transpose(cum[:, :h_b].reshape(batch, seq, h_b), (0, 2, 1)).reshape(batch, h_b, nq, 1, Q_TILE)
        o_a = prompt_attention("a", u, batch, seq, h_a, kv_a, o_qa, o_kva)
        o_b = prompt_attention("b", u, batch, seq, h_b, kv_b, o_qb, o_kvb, c_rows)
        o_c = prompt_attention("c", u, batch, seq, h_c, kv_c, o_qc, o_kvc, slopes)
        merged = branch_merge(xn, o_a, o_b, o_c, w_g, w_a16, w_b16, w_c16, l)
        hp = matmul(merged, w_o16, l, F32, resid=xp, name="out_proj")
        rows_p["kv_a"].append(u[:, o_kva:o_kva + kva_w].reshape(batch, seq, 2, kv_a, HEAD_DIM))
        rows_p["kv_b"].append(u[:, o_kvb:o_kvb + kvb_w].reshape(batch, seq, 2, kv_b, HEAD_DIM))
        rows_p["kv_c"].append(u[:, o_kvc:o_kvc + kvc_w].reshape(batch, seq, 2, kv_c, HEAD_DIM))
        rows_p["logf"].append(lf[:, :h_b].reshape(batch, seq, h_b))
        xn2, comb = moe_router(hp, g_norm_ffn[l], w_r3, b_r, l, n_groups, e_per_group)
        yp = moe_experts(xn2, comb, w_eg16, w_eu16, w_ed16, l, n_groups)
        xp = hp

        if ys is None:
            xn_s = rmsnorm(xs, g_norm_mix[l], BF16)
        else:
            xs, xn_s = add_rmsnorm(xs, ys, g_norm_mix[l], BF16, write_sum=True)
        us = matmul(xn_s, w_u, l, F32, name="in_proj_s")
        lf_s = logf_only(us, b_f_pad[l][None, :], o_f // LANES)[:, :h_b]
        q_of = lambda off, h: us[:, off:off + h * HEAD_DIM].reshape(dec_batch, h, HEAD_DIM)
        half = lambda off, w, part: us[:, off + part * (w // 2): off + (part + 1) * (w // 2)]
        o_as = decode_attention_a(q_of(o_qa, h_a), cache_a, l, page_table, kv_a)
        o_bs = decode_attention_b(q_of(o_qb, h_b), half(o_kvb, kvb_w, 0), half(o_kvb, kvb_w, 1), lf_s,
                                  cache_b, lf_cache_t, l, page_table, kv_b)
        o_cs = decode_attention_c(q_of(o_qc, h_c), half(o_kvc, kvc_w, 0), half(o_kvc, kvc_w, 1), slopes,
                                  cache_c, l, page_table, kv_c)
        merged_s = branch_merge(xn_s, o_as.astype(BF16), o_bs.astype(BF16), o_cs.astype(BF16),
                                w_g, w_a16, w_b16, w_c16, l)
        hs = matmul(merged_s, w_o16, l, F32, resid=xs, name="out_proj_s")
        rows_s["kv_a"].append(us[:, o_kva:o_kva + kva_w].reshape(dec_batch, 1, 2, kv_a, HEAD_DIM))
        rows_s["kv_b"].append(us[:, o_kvb:o_kvb + kvb_w].reshape(dec_batch, 1, 2, kv_b, HEAD_DIM))
        rows_s["kv_c"].append(us[:, o_kvc:o_kvc + kvc_w].reshape(dec_batch, 1, 2, kv_c, HEAD_DIM))
        rows_s["logf"].append(lf_s.reshape(dec_batch, 1, h_b))
        xn2_s, comb_s = moe_router(hs, g_norm_ffn[l], w_r3, b_r, l, n_groups, e_per_group)
        ys = moe_experts(xn2_s, comb_s, w_eg16, w_eu16, w_ed16, l, n_groups)
        xs = hs

    y_prompt = add_rmsnorm(xp, yp, g_norm_final, F32, write_sum=False).reshape(batch, seq, d_model)
    y_sample = add_rmsnorm(xs, ys, g_norm_final, F32, write_sum=False).reshape(dec_batch, 1, d_model)
    st = lambda xs_: jnp.stack(xs_, axis=0)
    return (y_prompt, y_sample, st(rows_p["kv_a"]), st(rows_p["kv_b"]), st(rows_p["logf"]), st(rows_p["kv_c"]),
            st(rows_s["kv_a"]), st(rows_s["kv_b"]), st(rows_s["logf"]), st(rows_s["kv_c"]))
```

```python
import functools

import jax
import jax.numpy as jnp
from jax import lax
from jax.experimental import pallas as pl
from jax.experimental.pallas import tpu as pltpu

F32 = jnp.float32
BF16 = jnp.bfloat16

HEAD_DIM = 128
LANES = 128
SUBLANES = 8
MXU_COLS = 256
VMEM_LIMIT_BIG = 56 * 1024 * 1024
VMEM_LIMIT_MID = 40 * 1024 * 1024
Q_TILE = 256
MOBA_BLOCK = 256
MOBA_TOPK = 3
ALIBI_MAX_BIAS = 8.0
TOPK_IN_GROUP = 2
RMS_EPS = 1e-6
NEG_BIG = -1e30
DEC_PAGES_PER_STEP = 8
MOE_TILE = 512
GATHER_ROWS = 128


def _cparams(sem, vmem=None):
    return pltpu.CompilerParams(dimension_semantics=sem, vmem_limit_bytes=vmem)


def _pick_tile(n, cap, mult):
    if n <= cap:
        return n
    best = None
    t = mult
    while t <= cap:
        if n % t == 0:
            best = t
        t += mult
    assert best is not None, (n, cap, mult)
    return best


def _split3(x):
    hi = x.astype(BF16)
    r = x - hi.astype(F32)
    mid = r.astype(BF16)
    lo = (r - mid.astype(F32)).astype(BF16)
    return hi, mid, lo


def _dot(a, b):
    return jnp.dot(a, b, preferred_element_type=F32)


def _dot_nt(a, b):
    return lax.dot_general(a, b, (((1,), (1,)), ((), ())), preferred_element_type=F32)


def _dot3(x, u):
    hi, mid, lo = _split3(x)
    return _dot(hi, u) + _dot(mid, u) + _dot(lo, u)


def _softplus(z):
    return jnp.maximum(z, 0.0) + jnp.log1p(jnp.exp(-jnp.abs(z)))


def _log_sigmoid(z):
    return jnp.minimum(z, 0.0) - jnp.log1p(jnp.exp(-jnp.abs(z)))


def _iota(shape, dim):
    return lax.broadcasted_iota(jnp.int32, shape, dim)


def _rowsum(x):
    return jnp.sum(x, axis=1, keepdims=True)


def _rowmax(x):
    return jnp.max(x, axis=1, keepdims=True)


def _rms_kernel(x_ref, g_ref, o_ref):
    x = x_ref[...]
    ms = jnp.mean(x * x, axis=-1, keepdims=True)
    o_ref[...] = ((x * lax.rsqrt(ms + RMS_EPS)) * g_ref[...]).astype(o_ref.dtype)


def rmsnorm(x, g, out_dtype):
    m, d = x.shape
    tm = _pick_tile(m, 256, 8)
    return pl.pallas_call(
        _rms_kernel,
        out_shape=jax.ShapeDtypeStruct((m, d), out_dtype),
        grid=(m // tm,),
        in_specs=[pl.BlockSpec((tm, d), lambda i: (i, 0)),
                  pl.BlockSpec((1, d), lambda i: (0, 0))],
        out_specs=pl.BlockSpec((tm, d), lambda i: (i, 0)),
        compiler_params=_cparams(("parallel",)),
        name="rmsnorm",
    )(x, g.reshape(1, d))


def _add_rms_kernel(a_ref, b_ref, g_ref, *out_refs):
    x = a_ref[...] + b_ref[...]
    ms = jnp.mean(x * x, axis=-1, keepdims=True)
    y = (x * lax.rsqrt(ms + RMS_EPS)) * g_ref[...]
    out_refs[-1][...] = y.astype(out_refs[-1].dtype)
    if len(out_refs) == 2:
        out_refs[0][...] = x


def add_rmsnorm(a, b, g, out_dtype, write_sum):
    m, d = a.shape
    tm = _pick_tile(m, 256, 8)
    row = pl.BlockSpec((tm, d), lambda i: (i, 0))
    out_shape = [jax.ShapeDtypeStruct((m, d), out_dtype)]
    out_specs = [row]
    if write_sum:
        out_shape.insert(0, jax.ShapeDtypeStruct((m, d), F32))
        out_specs.insert(0, row)
    res = pl.pallas_call(
        _add_rms_kernel,
        out_shape=tuple(out_shape),
        grid=(m // tm,),
        in_specs=[row, row, pl.BlockSpec((1, d), lambda i: (0, 0))],
        out_specs=tuple(out_specs),
        compiler_params=_cparams(("parallel",), VMEM_LIMIT_MID),
        name="add_rmsnorm",
    )(a, b, g.reshape(1, d))
    return res if write_sum else res[0]


def _mm_kernel(x_ref, w_ref, o_ref):
    o_ref[...] = _dot(x_ref[...], w_ref[...]).astype(o_ref.dtype)


def _mm_resid_kernel(x_ref, w_ref, r_ref, o_ref):
    o_ref[...] = (r_ref[...] + _dot(x_ref[...], w_ref[...])).astype(o_ref.dtype)


def matmul(x, w, layer, out_dtype, resid=None, name="matmul"):
    m, k = x.shape
    n = w.shape[2]
    tm = _pick_tile(m, 1024, 8)
    tn = _pick_tile(n, 1024, MXU_COLS)
    in_specs = [pl.BlockSpec((tm, k), lambda i, j: (i, 0)),
                pl.BlockSpec((None, k, tn), lambda i, j: (layer, 0, j))]
    args = [x, w]
    kern = _mm_kernel
    if resid is not None:
        in_specs.append(pl.BlockSpec((tm, tn), lambda i, j: (i, j)))
        args.append(resid)
        kern = _mm_resid_kernel
    return pl.pallas_call(
        kern,
        out_shape=jax.ShapeDtypeStruct((m, n), out_dtype),
        grid=(m // tm, n // tn),
        in_specs=in_specs,
        out_specs=pl.BlockSpec((tm, tn), lambda i, j: (i, j)),
        compiler_params=_cparams(("parallel", "parallel"), VMEM_LIMIT_BIG),
        name=name,
    )(*args)


def _logf_kernel(u_ref, b_ref, lf_ref):
    lf_ref[...] = _log_sigmoid(u_ref[...] + b_ref[...])


def logf_only(u, b_f_pad, col_block):
    m = u.shape[0]
    return pl.pallas_call(
        _logf_kernel,
        out_shape=jax.ShapeDtypeStruct((m, LANES), F32),
        grid=(1,),
        in_specs=[pl.BlockSpec((m, LANES), lambda i: (0, col_block)),
                  pl.BlockSpec((1, LANES), lambda i: (0, 0))],
        out_specs=pl.BlockSpec((m, LANES), lambda i: (0, 0)),
        name="logf_decode",
    )(u, b_f_pad)


def _logf_cum_kernel(u_ref, b_ref, lf_ref, c_ref, *, chunk):
    t = u_ref.shape[0]
    lf = _log_sigmoid(u_ref[...] + b_ref[...])
    lf_ref[...] = lf
    tri = (_iota((chunk, chunk), 0) >= _iota((chunk, chunk), 1)).astype(BF16)
    carry = jnp.zeros((1, LANES), F32)
    for n in range(t // chunk):
        hi, mid, lo = _split3(lf[n * chunk:(n + 1) * chunk])
        c = _dot(tri, hi) + _dot(tri, mid) + _dot(tri, lo) + carry
        c_ref[n * chunk:(n + 1) * chunk, :] = c
        carry = c[chunk - 1:chunk, :]


def logf_cumsum(u, b_f_pad, batch, seq, col_block):
    chunk = _pick_tile(seq, 256, 8)
    kern = functools.partial(_logf_cum_kernel, chunk=chunk)
    return pl.pallas_call(
        kern,
        out_shape=(jax.ShapeDtypeStruct((batch * seq, LANES), F32),
                   jax.ShapeDtypeStruct((batch * seq, LANES), F32)),
        grid=(batch,),
        in_specs=[pl.BlockSpec((seq, LANES), lambda b: (b, col_block)),
                  pl.BlockSpec((1, LANES), lambda b: (0, 0))],
        out_specs=(pl.BlockSpec((seq, LANES), lambda b: (b, 0)),
                   pl.BlockSpec((seq, LANES), lambda b: (b, 0))),
        compiler_params=_cparams(("parallel",)),
        name="logf_cumsum",
    )(u, b_f_pad)


def _stack_heads(q, g_heads):
    return jnp.concatenate([q[:, g * HEAD_DIM:(g + 1) * HEAD_DIM] for g in range(g_heads)], axis=0)


def _unstack_heads(o, g_heads, tq):
    return jnp.concatenate([o[g * tq:(g + 1) * tq] for g in range(g_heads)], axis=1)


def _tile_pos(g_heads, tq):
    row_t = jnp.concatenate([_iota((tq, tq), 0)] * g_heads, axis=0)
    col = _iota((g_heads * tq, tq), 1)
    return row_t, col


def _attn_a_kernel(q_ref, k_ref, v_ref, o_ref, *, tq, g_heads, scale):
    i = pl.program_id(2)
    rows = g_heads * tq
    qs = _stack_heads(q_ref[...], g_heads).astype(BF16)
    row_t, col = _tile_pos(g_heads, tq)
    suffix = (_iota((tq, tq), 0) >= _iota((tq, tq), 1)).astype(BF16)

    def block(j, carry, masked):
        o, c = carry
        start = pl.multiple_of(j * tq, tq)
        k = k_ref[pl.ds(start, tq), :].astype(BF16)
        v = v_ref[pl.ds(start, tq), :].astype(BF16)
        z = _dot_nt(qs, k) * scale
        sp = _softplus(z)
        if masked:
            reads = col < row_t
            sp = jnp.where(reads, sp, 0.0)
        hi = sp.astype(BF16)
        lo = (sp - hi.astype(F32)).astype(BF16)
        s_incl = c + _dot(hi, suffix) + _dot(lo, suffix)
        w = jnp.exp(z - s_incl)
        if masked:
            w = jnp.where(reads, w, 0.0)
        o = o + _dot(w.astype(BF16), v)
        return o, s_incl[:, 0:1]

    carry = (jnp.zeros((rows, HEAD_DIM), F32), jnp.zeros((rows, 1), F32))
    carry = block(i, carry, True)
    carry = lax.fori_loop(0, i, lambda s, cr: block(i - 1 - s, cr, False), carry)
    o_ref[...] = _unstack_heads(carry[0], g_heads, tq).astype(o_ref.dtype)


def _attn_b_kernel(q_ref, k_ref, v_ref, c_ref, o_ref, *, tq, g_heads, scale):
    i = pl.program_id(2)
    rows = g_heads * tq
    qs = _stack_heads(q_ref[...], g_heads).astype(BF16)
    row_t, col = _tile_pos(g_heads, tq)
    eye = _iota((tq, tq), 0) == _iota((tq, tq), 1)
    cq = jnp.concatenate(
        [jnp.sum(jnp.where(eye, c_ref[g, i], 0.0), axis=1, keepdims=True) for g in range(g_heads)],
        axis=0)

    def block(j, carry, masked):
        m, l, acc = carry
        start = pl.multiple_of(j * tq, tq)
        k = k_ref[pl.ds(start, tq), :].astype(BF16)
        v = v_ref[pl.ds(start, tq), :].astype(BF16)
        z = _dot_nt(qs, k) * scale
        ck = jnp.concatenate([jnp.broadcast_to(c_ref[g, j], (tq, tq)) for g in range(g_heads)], axis=0)
        z = (z + cq) - ck
        if masked:
            z = jnp.where(col <= row_t, z, NEG_BIG)
        m_new = jnp.maximum(m, jnp.max(z, axis=1, keepdims=True))
        alpha = jnp.exp(m - m_new)
        p = jnp.exp(z - m_new)
        l = alpha * l + jnp.sum(p, axis=1, keepdims=True)
        acc = alpha * acc + _dot(p.astype(BF16), v)
        return m_new, l, acc

    carry = (jnp.full((rows, 1), NEG_BIG, F32), jnp.zeros((rows, 1), F32),
             jnp.zeros((rows, HEAD_DIM), F32))
    carry = block(i, carry, True)
    carry = lax.fori_loop(0, i, lambda s, cr: block(s, cr, False), carry)
    o = carry[2] / carry[1]
    o_ref[...] = _unstack_heads(o, g_heads, tq).astype(o_ref.dtype)


def _topk_rank_lanes(gate, lane, n_valid_lanes):
    rank = jnp.zeros(gate.shape, F32)
    for n in range(n_valid_lanes):
        gn = gate[:, n:n + 1]
        beats = (gate > gn) | ((gate == gn) & (lane < n))
        cnt = jnp.sum(beats.astype(F32), axis=1, keepdims=True)
        rank = jnp.where(lane == n, cnt, rank)
    return rank


def _attn_c_kernel(slopes_ref, q_ref, k_ref, v_ref, o_ref, km_sc, m_sc, l_sc, acc_sc,
                   *, tq, g_heads, scale, nb):
    h = pl.program_id(1)
    i = pl.program_id(2)
    rows = g_heads * tq

    @pl.when(i == 0)
    def _():
        km_sc[...] = jnp.zeros(km_sc.shape, F32)
        for n in range(nb):
            km_sc[n:n + 1, :] = jnp.sum(k_ref[n * tq:(n + 1) * tq, :], axis=0, keepdims=True) * (1.0 / tq)

    qs = _stack_heads(q_ref[...], g_heads).astype(BF16)
    row_t, col = _tile_pos(g_heads, tq)
    slope = jnp.concatenate(
        [jnp.full((tq, 1), slopes_ref[h * g_heads + g], F32) for g in range(g_heads)], axis=0)

    lane = _iota((rows, LANES), 1)
    gate = _dot_nt(qs, km_sc[...].astype(BF16))
    gate = jnp.where(lane < i, gate, -jnp.inf)
    rank = _topk_rank_lanes(gate, lane, nb)
    sel = jnp.where((rank < float(MOBA_TOPK)) & (lane < i), 1.0, 0.0)

    def scores(n_start, k):
        z = _dot_nt(qs, k) * scale
        dist = ((i * tq - n_start) + (row_t - col)).astype(F32)
        return z - slope * dist

    start = pl.multiple_of(i * tq, tq)
    k = k_ref[pl.ds(start, tq), :].astype(BF16)
    v = v_ref[pl.ds(start, tq), :].astype(BF16)
    z = jnp.where(col <= row_t, scores(i * tq, k), NEG_BIG)
    m0 = jnp.max(z, axis=1, keepdims=True)
    p = jnp.exp(z - m0)
    m_sc[...] = m0
    l_sc[...] = jnp.sum(p, axis=1, keepdims=True)
    acc_sc[...] = _dot(p.astype(BF16), v)

    for n in range(nb):
        @pl.when(n < i)
        def _(n=n):
            k = k_ref[n * tq:(n + 1) * tq, :].astype(BF16)
            v = v_ref[n * tq:(n + 1) * tq, :].astype(BF16)
            take = sel[:, n:n + 1] > 0.0
            z = jnp.where(take, scores(n * tq, k), NEG_BIG)
            m_old = m_sc[...]
            m_new = jnp.maximum(m_old, jnp.max(z, axis=1, keepdims=True))
            alpha = jnp.exp(m_old - m_new)
            p = jnp.where(take, jnp.exp(z - m_new), 0.0)
            m_sc[...] = m_new
            l_sc[...] = alpha * l_sc[...] + jnp.sum(p, axis=1, keepdims=True)
            acc_sc[...] = alpha * acc_sc[...] + _dot(p.astype(BF16), v)

    o = acc_sc[...] / l_sc[...]
    o_ref[...] = _unstack_heads(o, g_heads, tq).astype(o_ref.dtype)


def prompt_attention(kind, u, batch, seq, n_q_heads, n_kv, q_off, kv_off, extra=None):
    g_heads = n_q_heads // n_kv
    tq = Q_TILE
    assert seq % tq == 0 and q_off % (g_heads * HEAD_DIM) == 0 and kv_off % HEAD_DIM == 0
    nq = seq // tq
    qw = g_heads * HEAD_DIM
    scale = HEAD_DIM ** -0.5
    q_spec = pl.BlockSpec((tq, qw), lambda b, h, i, *_: (b * nq + i, q_off // qw + h))
    k_spec = pl.BlockSpec((seq, HEAD_DIM), lambda b, h, i, *_: (b, kv_off // HEAD_DIM + h))
    v_spec = pl.BlockSpec((seq, HEAD_DIM), lambda b, h, i, *_: (b, kv_off // HEAD_DIM + n_kv + h))
    o_spec = pl.BlockSpec((tq, qw), lambda b, h, i, *_: (b * nq + i, h))
    out_shape = jax.ShapeDtypeStruct((batch * seq, n_q_heads * HEAD_DIM), BF16)
    grid = (batch, n_kv, nq)
    sem = ("parallel", "parallel", "arbitrary")
    if kind == "a":
        kern = functools.partial(_attn_a_kernel, tq=tq, g_heads=g_heads, scale=scale)
        return pl.pallas_call(
            kern, out_shape=out_shape, grid=grid,
            in_specs=[q_spec, k_spec, v_spec], out_specs=o_spec,
            compiler_params=_cparams(sem, VMEM_LIMIT_MID), name="attn_stickbreak",
        )(u, u, u)
    if kind == "b":
        c_rows = extra
        c_spec = pl.BlockSpec((None, g_heads, nq, 1, tq), lambda b, h, i: (b, h, 0, 0, 0))
        kern = functools.partial(_attn_b_kernel, tq=tq, g_heads=g_heads, scale=scale)
        return pl.pallas_call(
            kern, out_shape=out_shape, grid=grid,
            in_specs=[q_spec, k_spec, v_spec, c_spec], out_specs=o_spec,
            compiler_params=_cparams(sem, VMEM_LIMIT_MID), name="attn_forget",
        )(u, u, u, c_rows)
    slopes = extra
    nb = seq // MOBA_BLOCK
    assert tq == MOBA_BLOCK and nb <= LANES
    rows = g_heads * tq
    kern = functools.partial(_attn_c_kernel, tq=tq, g_heads=g_heads, scale=scale, nb=nb)
    return pl.pallas_call(
        kern, out_shape=out_shape,
        grid_spec=pltpu.PrefetchScalarGridSpec(
            num_scalar_prefetch=1, grid=grid,
            in_specs=[q_spec, k_spec, v_spec], out_specs=o_spec,
            scratch_shapes=[pltpu.VMEM((LANES, HEAD_DIM), F32), pltpu.VMEM((rows, 1), F32),
                            pltpu.VMEM((rows, 1), F32), pltpu.VMEM((rows, HEAD_DIM), F32)]),
        compiler_params=_cparams(sem, VMEM_LIMIT_MID), name="attn_moba",
    )(slopes, u, u, u)


def _head_major(n_kv):
    return n_kv % 4 != 0


def cache_pages(cache):
    depth, n_pool, page, two, n_kv, d = cache.shape
    if _head_major(n_kv):
        cache = jnp.transpose(cache, (0, 1, 2, 4, 3, 5))
    return cache.reshape(depth, n_pool, page * two * n_kv, d)


def _page_kv(page_ref, h, n_kv, page, which):
    row = 2 * h + which if _head_major(n_kv) else which * n_kv + h
    return page_ref[pl.ds(row, page, stride=2 * n_kv), :]


def _lanes(x):
    return jnp.broadcast_to(x, (x.shape[0], LANES))


def _split2(x):
    hi = x.astype(BF16)
    return hi, (x - hi.astype(F32)).astype(BF16)


def _row_parts(x):
    hi, mid, lo = _split3(x)
    return jnp.concatenate([hi, mid, lo], axis=0), jnp.concatenate([hi, mid], axis=0)


def _fold(a, rows):
    out = a[:rows]
    for i in range(1, a.shape[0] // rows):
        out = out + a[i * rows:(i + 1) * rows]
    return out


def _mm_hp(x3, x2, w):
    rows = x2.shape[0] // 2
    wh, wm = _split2(w)
    return _fold(_dot(x3, wh), rows) + _fold(_dot(x2, wm), rows)


def _qk_hp(q3, q2, k):
    rows = q2.shape[0] // 2
    kh, km = _split2(k)
    return _fold(_dot_nt(q3, kh), rows) + _fold(_dot_nt(q2, km), rows)


def _pv_hp(p, v):
    rows = p.shape[0]
    ph, pm = _split2(p)
    vh, vm = _split2(v)
    return _fold(_dot(jnp.concatenate([ph, pm], axis=0), vh), rows) + _dot(ph, vm)


def _page_keys(page_ref, n_kv, page):
    return jnp.concatenate([_page_kv(page_ref, h, n_kv, page, 0) for h in range(n_kv)], axis=0)


def _page_values(page_ref, n_kv, page):
    return jnp.concatenate([_page_kv(page_ref, h, n_kv, page, 1) for h in range(n_kv)], axis=1)


def _own_head(x, n_kv, g_heads, width):
    row = _iota((x.shape[0], width), 0)
    out = jnp.zeros((x.shape[0], width), F32)
    for h in range(n_kv):
        mine = (row >= h * g_heads) & (row < (h + 1) * g_heads)
        out = out + jnp.where(mine, x[:, h * width:(h + 1) * width], 0.0)
    return out


def _scores(q3, q2, page_ref, n_kv, g_heads, page, scale):
    return _own_head(_qk_hp(q3, q2, _page_keys(page_ref, n_kv, page)), n_kv, g_heads, page) * scale


def _new_token_score(q, knew_ref, n_kv, g_heads, scale):
    row = _iota((q.shape[0], 1), 0)
    out = jnp.zeros((q.shape[0], 1), F32)
    for h in range(n_kv):
        mine = (row >= h * g_heads) & (row < (h + 1) * g_heads)
        out = out + jnp.where(mine, _rowsum(q * knew_ref[h]), 0.0)
    return out * scale


def _dec_a_kernel(pt_ref, q_ref, *rest, n_pages_step, n_kv, g_heads, page, scale):
    page_refs = rest[:n_pages_step]
    o_ref, acc_sc, c_sc = rest[n_pages_step:]
    j = pl.program_id(1)
    rows = q_ref.shape[0]

    @pl.when(j == 0)
    def _():
        acc_sc[...] = jnp.zeros(acc_sc.shape, F32)
        c_sc[...] = jnp.zeros(c_sc.shape, F32)

    q3, q2 = _row_parts(q_ref[...])
    suffix = (_iota((page, page), 0) >= _iota((page, page), 1)).astype(BF16)
    zs = [_scores(q3, q2, page_refs[r], n_kv, g_heads, page, scale) for r in range(n_pages_step)]
    sps = [_split2(_softplus(z)) for z in zs]
    stacked = jnp.concatenate([hi for hi, _ in sps] + [lo for _, lo in sps], axis=0)
    sums = _dot(stacked, suffix)
    c = c_sc[:, 0:1]
    acc = acc_sc[...]
    for r in range(n_pages_step):
        local = sums[r * rows:(r + 1) * rows] + sums[(n_pages_step + r) * rows:(n_pages_step + r + 1) * rows]
        s_incl = c + local
        acc = acc + _pv_hp(jnp.exp(zs[r] - s_incl), _page_values(page_refs[r], n_kv, page))
        c = s_incl[:, 0:1]
    acc_sc[...] = acc
    c_sc[...] = _lanes(c)

    @pl.when(j == pl.num_programs(1) - 1)
    def _():
        o_ref[...] = _own_head(acc, n_kv, g_heads, HEAD_DIM)


def _dec_b_kernel(pt_ref, q_ref, knew_ref, vnew_ref, lfnew_ref, *rest, n_pages_step, n_kv, g_heads, page, scale):
    page_refs = rest[:n_pages_step]
    lf_refs = rest[n_pages_step:2 * n_pages_step]
    o_ref, m_sc, l_sc, acc_sc, cb_sc = rest[2 * n_pages_step:]
    j = pl.program_id(1)
    q = q_ref[...]
    rows = q.shape[0]

    @pl.when(j == 0)
    def _():
        m_sc[...] = _lanes(_new_token_score(q, knew_ref, n_kv, g_heads, scale))
        l_sc[...] = jnp.ones(l_sc.shape, F32)
        acc_sc[...] = jnp.broadcast_to(vnew_ref[...], acc_sc.shape)
        cb_sc[...] = lfnew_ref[...]

    q3, q2 = _row_parts(q)
    after = (_iota((page, page), 0) > _iota((page, page), 1)).astype(BF16)
    lfs = [lf_refs[r][...] for r in range(n_pages_step)]
    later_all = _dot3(jnp.concatenate(lfs, axis=0), after)
    cb = cb_sc[:, 0:1]
    zs = []
    for r in range(n_pages_step):
        later = later_all[r * rows:(r + 1) * rows]
        z = _scores(q3, q2, page_refs[r], n_kv, g_heads, page, scale)
        zs.append(z + (cb + later))
        cb = cb + (later[:, 0:1] + lfs[r][:, 0:1])
    m_old = m_sc[:, 0:1]
    m_new = jnp.maximum(m_old, _rowmax(functools.reduce(jnp.maximum, zs)))
    alpha = jnp.exp(m_old - m_new)
    ps = [jnp.exp(z - m_new) for z in zs]
    l_new = alpha * l_sc[:, 0:1] + _rowsum(functools.reduce(jnp.add, ps))
    acc = alpha * acc_sc[...]
    for r in range(n_pages_step):
        acc = acc + _pv_hp(ps[r], _page_values(page_refs[r], n_kv, page))
    m_sc[...] = _lanes(m_new)
    l_sc[...] = _lanes(l_new)
    acc_sc[...] = acc
    cb_sc[...] = _lanes(cb)

    @pl.when(j == pl.num_programs(1) - 1)
    def _():
        o_ref[...] = _own_head(acc, n_kv, g_heads, HEAD_DIM) / l_new


def _dec_c_kernel(pt_ref, q_ref, knew_ref, vnew_ref, slope_ref, *rest,
                  n_blocks_step, n_pages_blk, n_kv, g_heads, page, scale, past_len, n_blocks):
    n_page_refs = n_blocks_step * n_pages_blk
    page_refs = rest[:n_page_refs]
    o_ref, part_sc, g_sc, m_sc, l_sc = rest[n_page_refs:]
    j = pl.program_id(1)
    q = q_ref[...]
    rows = q.shape[0]
    lane = _iota((rows, LANES), 1)
    n_keys = n_pages_blk * page

    @pl.when(j == 0)
    def _():
        g_sc[...] = jnp.zeros(g_sc.shape, F32)
        m_sc[...] = jnp.zeros(m_sc.shape, F32)
        l_sc[...] = jnp.zeros(l_sc.shape, F32)

    q3, q2 = _row_parts(q)
    slope = slope_ref[:, 0:1]
    row1 = _iota((rows, 1), 0)
    for bb in range(n_blocks_step):
        n = j * n_blocks_step + bb
        zs = []
        g_n = jnp.zeros((rows, 1), F32)
        refs = page_refs[bb * n_pages_blk:(bb + 1) * n_pages_blk]
        for r, page_ref in enumerate(refs):
            pos = n * n_keys + r * page + _iota((1, page), 1)
            dist = (past_len - pos).astype(F32)
            zs.append(_scores(q3, q2, page_ref, n_kv, g_heads, page, scale) - slope * dist)
        for h in range(n_kv):
            ksum = functools.reduce(jnp.add, [jnp.sum(_page_kv(pr, h, n_kv, page, 0), axis=0, keepdims=True)
                                              for pr in refs])
            mine = (row1 >= h * g_heads) & (row1 < (h + 1) * g_heads)
            g_n = g_n + jnp.where(mine, _rowsum(q * (ksum * (1.0 / n_keys))), 0.0)
        m_n = _rowmax(functools.reduce(jnp.maximum, zs))
        ps = [jnp.exp(z - m_n) for z in zs]
        l_n = _rowsum(functools.reduce(jnp.add, ps))
        part_sc[n] = functools.reduce(
            jnp.add, [_pv_hp(p, _page_values(pr, n_kv, page)) for p, pr in zip(ps, refs)])
        g_sc[...] = jnp.where(lane == n, g_n, g_sc[...])
        m_sc[...] = jnp.where(lane == n, m_n, m_sc[...])
        l_sc[...] = jnp.where(lane == n, l_n, l_sc[...])

    @pl.when(j == pl.num_programs(1) - 1)
    def _():
        gate = jnp.where(lane < n_blocks, g_sc[...], -jnp.inf)
        rank = _topk_rank_lanes(gate, lane, n_blocks)
        sel = (rank < float(MOBA_TOPK)) & (lane < n_blocks)
        z_own = _new_token_score(q, knew_ref, n_kv, g_heads, scale)
        m_star = jnp.maximum(z_own, _rowmax(jnp.where(sel, m_sc[...], NEG_BIG)))
        wts = jnp.where(sel, jnp.exp(m_sc[...] - m_star), 0.0)
        w_own = jnp.exp(z_own - m_star)
        l_tot = w_own + _rowsum(wts * l_sc[...])
        o = w_own * jnp.broadcast_to(vnew_ref[...], (rows, vnew_ref.shape[1]))
        for n in range(n_blocks):
            o = o + wts[:, n:n + 1] * part_sc[n]
        o_ref[...] = _own_head(o, n_kv, g_heads, HEAD_DIM) / l_tot


def _head_rows(n_q_heads):
    return -(-n_q_heads // SUBLANES) * SUBLANES


def _pad_heads(x):
    rows = _head_rows(x.shape[1])
    return jnp.pad(x, ((0, 0), (0, rows - x.shape[1])) + ((0, 0),) * (x.ndim - 2))


def _decode_call(kern, name, page_table, n_steps, rows, per_seq_inputs, shared_inputs, paged, scratch):
    s = page_table.shape[0]
    in_specs = [pl.BlockSpec((None,) + x.shape[1:], lambda b, j, pt, nd=x.ndim: (b,) + (0,) * (nd - 1))
                for x in per_seq_inputs]
    in_specs += [pl.BlockSpec(x.shape, lambda b, j, pt, nd=x.ndim: (0,) * nd) for x in shared_inputs]
    args = list(per_seq_inputs) + list(shared_inputs)
    for arr, layer, page_fn in paged:
        in_specs.append(pl.BlockSpec(
            (None, None) + arr.shape[2:],
            lambda b, j, pt, layer=layer, page_fn=page_fn: (layer, page_fn(b, j, pt), 0, 0)))
        args.append(arr)
    return pl.pallas_call(
        kern,
        out_shape=jax.ShapeDtypeStruct((s, rows, HEAD_DIM), F32),
        grid_spec=pltpu.PrefetchScalarGridSpec(
            num_scalar_prefetch=1, grid=(s, n_steps),
            in_specs=in_specs,
            out_specs=pl.BlockSpec((None, rows, HEAD_DIM), lambda b, j, pt: (b, 0, 0)),
            scratch_shapes=scratch),
        compiler_params=_cparams(("parallel", "arbitrary"), VMEM_LIMIT_BIG),
        name=name,
    )(page_table, *args)


def _state(rows):
    return pltpu.VMEM((rows, LANES), F32)


def decode_attention_a(q, pages, layer, page_table, n_kv):
    s, h, _ = q.shape
    rows = _head_rows(h)
    n_pages = page_table.shape[1]
    page = pages.shape[2] // (2 * n_kv)
    pps = _pick_tile(n_pages, DEC_PAGES_PER_STEP, 1)
    latest_first = lambda r: (lambda b, j, pt: pt[b, n_pages - 1 - (j * pps + r)])
    kern = functools.partial(_dec_a_kernel, n_pages_step=pps, n_kv=n_kv, g_heads=h // n_kv, page=page,
                             scale=HEAD_DIM ** -0.5)
    out = _decode_call(kern, "decode_stickbreak", page_table, n_pages // pps, rows,
                       [_pad_heads(q)], [],
                       [(pages, layer, latest_first(r)) for r in range(pps)],
                       [pltpu.VMEM((rows, n_kv * HEAD_DIM), F32), _state(rows)])
    return out[:, :h].reshape(s, h * HEAD_DIM)


def decode_attention_b(q, k_new, v_new, lf_new, pages, lf_pages, layer, page_table, n_kv):
    s, h, _ = q.shape
    rows = _head_rows(h)
    n_pages = page_table.shape[1]
    page = pages.shape[2] // (2 * n_kv)
    pps = _pick_tile(n_pages, DEC_PAGES_PER_STEP, 1)
    latest_first = lambda r: (lambda b, j, pt: pt[b, n_pages - 1 - (j * pps + r)])
    lfn = jnp.broadcast_to(_pad_heads(lf_new)[..., None], (s, rows, LANES))
    kern = functools.partial(_dec_b_kernel, n_pages_step=pps, n_kv=n_kv, g_heads=h // n_kv, page=page,
                             scale=HEAD_DIM ** -0.5)
    out = _decode_call(kern, "decode_forget", page_table, n_pages // pps, rows,
                       [_pad_heads(q), k_new[:, :, None, :], v_new.reshape(s, 1, n_kv * HEAD_DIM), lfn], [],
                       [(pages, layer, latest_first(r)) for r in range(pps)]
                       + [(lf_pages, layer, latest_first(r)) for r in range(pps)],
                       [_state(rows), _state(rows), pltpu.VMEM((rows, n_kv * HEAD_DIM), F32), _state(rows)])
    return out[:, :h].reshape(s, h * HEAD_DIM)


def decode_attention_c(q, k_new, v_new, slopes, pages, layer, page_table, n_kv):
    s, h, _ = q.shape
    rows = _head_rows(h)
    n_pages = page_table.shape[1]
    page = pages.shape[2] // (2 * n_kv)
    assert MOBA_BLOCK % page == 0
    ppb = MOBA_BLOCK // page
    assert n_pages % ppb == 0
    n_blocks = n_pages // ppb
    assert n_blocks <= LANES
    bps = _pick_tile(n_blocks, max(DEC_PAGES_PER_STEP // ppb, 1), 1)
    pps = bps * ppb
    slope_rows = jnp.broadcast_to(_pad_heads(slopes[None])[0][:, None], (rows, LANES))
    in_order = lambda r: (lambda b, j, pt: pt[b, j * pps + r])
    kern = functools.partial(_dec_c_kernel, n_blocks_step=bps, n_pages_blk=ppb, n_kv=n_kv, g_heads=h // n_kv,
                             page=page, scale=HEAD_DIM ** -0.5, past_len=n_pages * page, n_blocks=n_blocks)
    out = _decode_call(kern, "decode_moba", page_table, n_blocks // bps, rows,
                       [_pad_heads(q), k_new[:, :, None, :], v_new.reshape(s, 1, n_kv * HEAD_DIM)], [slope_rows],
                       [(pages, layer, in_order(r)) for r in range(pps)],
                       [pltpu.VMEM((n_blocks, rows, n_kv * HEAD_DIM), F32)] + [_state(rows)] * 3)
    return out[:, :h].reshape(s, h * HEAD_DIM)


def _mm_hp_kernel(x_ref, w_ref, *rest):
    x3, x2 = _row_parts(x_ref[...])
    out = _mm_hp(x3, x2, w_ref[...])
    if len(rest) == 2:
        out = out + rest[0][...]
    rest[-1][...] = out


def matmul_hp(x, w, layer, resid=None, name="matmul_hp"):
    m, k = x.shape
    n = w.shape[2]
    tn = _pick_tile(n, 512, MXU_COLS)
    in_specs = [pl.BlockSpec((m, k), lambda j: (0, 0)),
                pl.BlockSpec((None, k, tn), lambda j: (layer, 0, j))]
    args = [x, w]
    if resid is not None:
        in_specs.append(pl.BlockSpec((m, tn), lambda j: (0, j)))
        args.append(resid)
    return pl.pallas_call(
        _mm_hp_kernel,
        out_shape=jax.ShapeDtypeStruct((m, n), F32),
        grid=(n // tn,),
        in_specs=in_specs,
        out_specs=pl.BlockSpec((m, tn), lambda j: (0, j)),
        compiler_params=_cparams(("parallel",), VMEM_LIMIT_BIG),
        name=name,
    )(*args)


def _merge_hp_kernel(xn_ref, oa_ref, ob_ref, oc_ref, wg_ref, wa_ref, wb_ref, wc_ref, o_ref):
    x3, x2 = _row_parts(xn_ref[...])
    acc = None
    for br, (o_r, w_r) in enumerate(((oa_ref, wa_ref), (ob_ref, wb_ref), (oc_ref, wc_ref))):
        gate = jax.nn.sigmoid(_mm_hp(x3, x2, wg_ref[br]))
        o3, o2 = _row_parts(o_r[...])
        term = gate * _mm_hp(o3, o2, w_r[...])
        acc = term if acc is None else acc + term
    o_ref[...] = acc


def branch_merge_hp(xn, o_a, o_b, o_c, wg, w_a, w_b, w_c, layer):
    m, d = xn.shape
    tn = _pick_tile(d, 256, LANES)
    act = lambda a: pl.BlockSpec(a.shape, lambda j: (0, 0))
    wsp = lambda w: pl.BlockSpec((None, w.shape[1], tn), lambda j: (layer, 0, j))
    return pl.pallas_call(
        _merge_hp_kernel,
        out_shape=jax.ShapeDtypeStruct((m, d), F32),
        grid=(d // tn,),
        in_specs=[act(xn), act(o_a), act(o_b), act(o_c),
                  pl.BlockSpec((None, 3, d, tn), lambda j: (layer, 0, 0, j)),
                  wsp(w_a), wsp(w_b), wsp(w_c)],
        out_specs=pl.BlockSpec((m, tn), lambda j: (0, j)),
        compiler_params=_cparams(("parallel",), VMEM_LIMIT_BIG),
        name="branch_merge_hp",
    )(xn, o_a, o_b, o_c, wg, w_a, w_b, w_c)


def _experts_hp_kernel(eid_ref, ok_ref, x_ref, comb_ref, wg_ref, wu_ref, wd_ref, o_ref, *, n_groups):
    s = pl.program_id(0)
    f = pl.program_id(1)

    @pl.when((s == 0) & (f == 0))
    def _():
        o_ref[...] = jnp.zeros(o_ref.shape, F32)

    @pl.when(ok_ref[s] > 0)
    def _():
        x3, x2 = _row_parts(x_ref[...])
        lane = _iota(comb_ref.shape, 1)
        w_tok = _rowsum(jnp.where(lane == n_groups + eid_ref[s], comb_ref[...], 0.0))
        hg = _mm_hp(x3, x2, wg_ref[...])
        hu = _mm_hp(x3, x2, wu_ref[...])
        hid = (hg * jax.nn.sigmoid(hg)) * hu * w_tok
        h3, h2 = _row_parts(hid)
        o_ref[...] += _mm_hp(h3, h2, wd_ref[...])


def moe_experts_hp(xn, comb, w_eg, w_eu, w_ed, layer, n_groups):
    m, d = xn.shape
    n_experts, d_ff = w_eg.shape[1], w_eg.shape[3]
    tf = _pick_tile(d_ff, 256, LANES)
    nf = d_ff // tf
    active = jnp.any(comb[:, n_groups:n_groups + n_experts] > 0.0, axis=0)
    order = jnp.argsort(jnp.logical_not(active), stable=True).astype(jnp.int32)
    n_active = jnp.sum(active.astype(jnp.int32))
    step = jnp.arange(n_experts, dtype=jnp.int32)
    ok = (step < n_active).astype(jnp.int32)
    eid = order[jnp.minimum(step, jnp.maximum(n_active - 1, 0))]

    def fidx(s, f, eid_ref, ok_ref):
        return jnp.where(ok_ref[s] > 0, f, nf - 1)

    kern = functools.partial(_experts_hp_kernel, n_groups=n_groups)
    return pl.pallas_call(
        kern,
        out_shape=jax.ShapeDtypeStruct((m, d), F32),
        grid_spec=pltpu.PrefetchScalarGridSpec(
            num_scalar_prefetch=2, grid=(n_experts, nf),
            in_specs=[pl.BlockSpec((m, d), lambda s, f, e, k: (0, 0)),
                      pl.BlockSpec((m, LANES), lambda s, f, e, k: (0, 0)),
                      pl.BlockSpec((None, None, d, tf), lambda s, f, e, k: (layer, e[s], 0, fidx(s, f, e, k))),
                      pl.BlockSpec((None, None, d, tf), lambda s, f, e, k: (layer, e[s], 0, fidx(s, f, e, k))),
                      pl.BlockSpec((None, None, tf, d), lambda s, f, e, k: (layer, e[s], fidx(s, f, e, k), 0))],
            out_specs=pl.BlockSpec((m, d), lambda s, f, e, k: (0, 0))),
        compiler_params=_cparams(("arbitrary", "arbitrary"), VMEM_LIMIT_BIG),
        name="moe_experts_hp",
    )(eid, ok, xn, comb, w_eg, w_eu, w_ed)


def _merge_kernel(xn_ref, oa_ref, ob_ref, oc_ref, wg_ref, wa_ref, wb_ref, wc_ref, o_ref):
    xn = xn_ref[...]
    acc = None
    for br, (o_r, w_r) in enumerate(((oa_ref, wa_ref), (ob_ref, wb_ref), (oc_ref, wc_ref))):
        gate = jax.nn.sigmoid(_dot(xn, wg_ref[br]))
        term = gate * _dot(o_r[...], w_r[...])
        acc = term if acc is None else acc + term
    o_ref[...] = acc.astype(o_ref.dtype)


def branch_merge(xn, o_a, o_b, o_c, wg, w_a, w_b, w_c, layer):
    m, d = xn.shape
    tm = _pick_tile(m, 512, 8)
    tn = _pick_tile(d, 256, LANES)
    act = lambda a: pl.BlockSpec((tm, a.shape[1]), lambda i, j: (i, 0))
    wsp = lambda w: pl.BlockSpec((None, w.shape[1], tn), lambda i, j: (layer, 0, j))
    return pl.pallas_call(
        _merge_kernel,
        out_shape=jax.ShapeDtypeStruct((m, d), BF16),
        grid=(m // tm, d // tn),
        in_specs=[act(xn), act(o_a), act(o_b), act(o_c),
                  pl.BlockSpec((None, 3, d, tn), lambda i, j: (layer, 0, 0, j)),
                  wsp(w_a), wsp(w_b), wsp(w_c)],
        out_specs=pl.BlockSpec((tm, tn), lambda i, j: (i, j)),
        compiler_params=_cparams(("parallel", "parallel"), VMEM_LIMIT_BIG),
        name="branch_merge",
    )(xn, o_a, o_b, o_c, wg, w_a, w_b, w_c)


def _router_kernel(x_ref, g_ref, w_ref, b_ref, o_ref, *, n_groups, e_per_group):
    x = x_ref[...]
    d = x.shape[1]
    ms = jnp.mean(x * x, axis=-1, keepdims=True)
    xn = (x * lax.rsqrt(ms + RMS_EPS)) * g_ref[...]
    xh, xm, xl = _split3(xn)
    wh, wm, wl = w_ref[0], w_ref[1], w_ref[2]
    logits = (_dot(xh, wh) + (_dot(xh, wm) + _dot(xm, wh))
              + (_dot(xh, wl) + _dot(xm, wm) + _dot(xl, wh))) + b_ref[...]
    rows = logits.shape[0]
    lane = _iota((rows, LANES), 1).astype(F32)
    n_experts = n_groups * e_per_group
    is_g = lane < n_groups
    lg = jnp.where(is_g, logits, -jnp.inf)
    mg = jnp.max(lg, axis=1, keepdims=True)
    grp = jnp.min(jnp.where(is_g & (lg == mg), lane, float(LANES)), axis=1, keepdims=True)
    p_grp = 1.0 / jnp.sum(jnp.exp(lg - mg), axis=1, keepdims=True)
    e_lo = n_groups + grp * e_per_group
    in_grp = (lane >= e_lo) & (lane < e_lo + e_per_group)
    le = jnp.where(in_grp, logits, -jnp.inf)
    me = jnp.max(le, axis=1, keepdims=True)
    ee = jnp.exp(le - me)
    pe = ee / jnp.sum(ee, axis=1, keepdims=True)
    rank = jnp.zeros((rows, LANES), F32)
    for e in range(n_experts):
        ln = n_groups + e
        pv = pe[:, ln:ln + 1]
        beats = in_grp & ((pe > pv) | ((pe == pv) & (lane < ln)))
        cnt = jnp.sum(beats.astype(F32), axis=1, keepdims=True)
        rank = jnp.where(lane == ln, cnt, rank)
    top = in_grp & (rank < float(TOPK_IN_GROUP))
    top_p = jnp.where(top, pe, 0.0)
    top_p = top_p / jnp.sum(top_p, axis=1, keepdims=True)
    o_ref[:, :d] = xn
    o_ref[:, d:] = jnp.where(lane == n_groups + n_experts, grp, p_grp * top_p)


def moe_router(h, g, w_r3, b_r, layer, n_groups, e_per_group):
    m, d = h.shape
    tm = _pick_tile(m, 256, 8)
    kern = functools.partial(_router_kernel, n_groups=n_groups, e_per_group=e_per_group)
    return pl.pallas_call(
        kern,
        out_shape=jax.ShapeDtypeStruct((m, d + LANES), F32),
        grid=(m // tm,),
        in_specs=[pl.BlockSpec((tm, d), lambda i: (i, 0)),
                  pl.BlockSpec((1, d), lambda i: (0, 0)),
                  pl.BlockSpec((None, 3, d, LANES), lambda i: (layer, 0, 0, 0)),
                  pl.BlockSpec((None, 1, LANES), lambda i: (layer, 0, 0))],
        out_specs=pl.BlockSpec((tm, d + LANES), lambda i: (i, 0)),
        compiler_params=_cparams(("parallel",), VMEM_LIMIT_MID),
        name="moe_router",
    )(h, g.reshape(1, d), w_r3, b_r)


def _gather_kernel(idx_ref, src_hbm, *rest, rows, splits):
    out_refs = rest[:len(splits)]
    buf, sem = rest[len(splits):]
    i = pl.program_id(0)
    slot = lax.rem(i, 2)

    def row_copy(r, src_row, slot):
        return pltpu.make_async_copy(src_hbm.at[pl.ds(src_row, 1), :], buf.at[slot, pl.ds(r, 1), :], sem.at[slot])

    def start_tile(tile, slot):
        for r in range(rows):
            row_copy(r, idx_ref[tile * rows + r], slot).start()

    @pl.when(i == 0)
    def _():
        start_tile(0, 0)

    @pl.when(i + 1 < pl.num_programs(0))
    def _():
        start_tile(i + 1, 1 - slot)

    for r in range(rows):
        row_copy(r, 0, slot).wait()
    for o_ref, (lo, hi) in zip(out_refs, splits):
        o_ref[...] = buf[slot, :, lo:hi].astype(o_ref.dtype)


def gather_rows(src, idx, splits, dtypes):
    n_out = idx.shape[0]
    w = src.shape[1]
    rows = GATHER_ROWS
    assert n_out % rows == 0
    kern = functools.partial(_gather_kernel, rows=rows, splits=tuple(splits))
    res = pl.pallas_call(
        kern,
        out_shape=tuple(jax.ShapeDtypeStruct((n_out, hi - lo), dt) for (lo, hi), dt in zip(splits, dtypes)),
        grid_spec=pltpu.PrefetchScalarGridSpec(
            num_scalar_prefetch=1, grid=(n_out // rows,),
            in_specs=[pl.BlockSpec(memory_space=pl.ANY)],
            out_specs=tuple(pl.BlockSpec((rows, hi - lo), lambda i, idx_ref: (i, 0)) for lo, hi in splits),
            scratch_shapes=[pltpu.VMEM((2, rows, w), F32), pltpu.SemaphoreType.DMA((2,))]),
        compiler_params=_cparams(("arbitrary",)),
        name="gather_rows",
    )(idx, src)
    return res


def _group_experts_kernel(tg_ref, tv_ref, x_ref, comb_ref, wg_ref, wu_ref, wd_ref, o_ref,
                          *, n_groups, e_per_group):
    i = pl.program_id(0)
    e = pl.program_id(1)
    f = pl.program_id(2)

    @pl.when((e == 0) & (f == 0))
    def _():
        o_ref[...] = jnp.zeros(o_ref.shape, F32)

    @pl.when(tv_ref[i] > 0)
    def _():
        x = x_ref[...]
        lane = _iota(comb_ref.shape, 1)
        e_lane = n_groups + tg_ref[i] * e_per_group + e
        w_tok = jnp.sum(jnp.where(lane == e_lane, comb_ref[...], 0.0), axis=1, keepdims=True)
        hg = _dot(x, wg_ref[...])
        hu = _dot(x, wu_ref[...])
        hid = (hg * jax.nn.sigmoid(hg)) * hu * w_tok
        o_ref[...] += _dot(hid.astype(BF16), wd_ref[...])


def moe_group_experts(xs, combs, tile_group, tile_valid, w_eg, w_eu, w_ed, layer, n_groups, e_per_group):
    m, d = xs.shape
    d_ff = w_eg.shape[3]
    tm = MOE_TILE
    tf = _pick_tile(d_ff, 512, LANES)
    nf = d_ff // tf

    def w_idx(i, e, f, tg, tv):
        ok = tv[i] > 0
        return tg[i] * e_per_group + jnp.where(ok, e, e_per_group - 1), jnp.where(ok, f, nf - 1)

    def wcol(i, e, f, tg, tv):
        ex, fx = w_idx(i, e, f, tg, tv)
        return (layer, ex, 0, fx)

    def wrow(i, e, f, tg, tv):
        ex, fx = w_idx(i, e, f, tg, tv)
        return (layer, ex, fx, 0)

    kern = functools.partial(_group_experts_kernel, n_groups=n_groups, e_per_group=e_per_group)
    return pl.pallas_call(
        kern,
        out_shape=jax.ShapeDtypeStruct((m, d), F32),
        grid_spec=pltpu.PrefetchScalarGridSpec(
            num_scalar_prefetch=2, grid=(m // tm, e_per_group, nf),
            in_specs=[pl.BlockSpec((tm, d), lambda i, e, f, tg, tv: (i, 0)),
                      pl.BlockSpec((tm, LANES), lambda i, e, f, tg, tv: (i, 0)),
                      pl.BlockSpec((None, None, d, tf), wcol),
                      pl.BlockSpec((None, None, d, tf), wcol),
                      pl.BlockSpec((None, None, tf, d), wrow)],
            out_specs=pl.BlockSpec((tm, d), lambda i, e, f, tg, tv: (i, 0))),
        compiler_params=_cparams(("parallel", "arbitrary", "arbitrary"), VMEM_LIMIT_BIG),
        name="moe_group_experts",
    )(tile_group, tile_valid, xs, combs, w_eg, w_eu, w_ed)


def moe_sorted(h, g, w_r3, b_r, w_eg, w_eu, w_ed, layer, n_groups, e_per_group):
    m, d = h.shape
    tm = MOE_TILE
    n_experts = n_groups * e_per_group
    packed = moe_router(h, g, w_r3, b_r, layer, n_groups, e_per_group)
    grp = packed[:, d + n_groups + n_experts].astype(jnp.int32)
    onehot = (grp[:, None] == jnp.arange(n_groups)[None, :]).astype(jnp.int32)
    counts = jnp.sum(onehot, axis=0)
    rank = jnp.sum((jnp.cumsum(onehot, axis=0) - onehot) * onehot, axis=1)
    padded = ((counts + tm - 1) // tm) * tm
    ends = jnp.cumsum(padded)
    starts = ends - padded
    pos = starts[grp] + rank
    cap = m + n_groups * tm
    perm = jnp.zeros((cap,), jnp.int32).at[pos].set(jnp.arange(m, dtype=jnp.int32))
    tile_start = jnp.arange(cap // tm, dtype=jnp.int32) * tm
    tile_group = jnp.minimum(jnp.searchsorted(ends, tile_start, side="right"), n_groups - 1).astype(jnp.int32)
    tile_valid = (tile_start < ends[-1]).astype(jnp.int32)
    xs, combs = gather_rows(packed, perm, [(0, d), (d, d + LANES)], [BF16, F32])
    ys = moe_group_experts(xs, combs, tile_group, tile_valid, w_eg, w_eu, w_ed, layer, n_groups, e_per_group)
    (y,) = gather_rows(ys, pos, [(0, d)], [F32])
    return y


def kernel(x_prompt, x_sample, cache_kv_a, cache_kv_b, cache_logf_b, cache_kv_c, page_table, w_in, b_f, w_br_a, w_br_b, w_br_c, w_o, g_norm_mix, g_norm_ffn, w_router_group, b_router_group, w_router_expert, b_router_expert, w_exp_gate, w_exp_up, w_exp_down, g_norm_final):
    batch, seq, d_model = x_prompt.shape
    dec_batch, dec_seq, _ = x_sample.shape
    assert dec_seq == 1
    depth = w_in.shape[0]
    kv_a, kv_b, kv_c = cache_kv_a.shape[4], cache_kv_b.shape[4], cache_kv_c.shape[4]
    h_a, h_b, h_c = (w.shape[1] // HEAD_DIM for w in (w_br_a, w_br_b, w_br_c))
    assert cache_logf_b.shape[3] == h_b
    n_groups = w_router_group.shape[2]
    n_experts = w_router_expert.shape[2]
    e_per_group = n_experts // n_groups
    assert n_groups + n_experts < LANES

    qa_w, kva_w = h_a * HEAD_DIM, 2 * kv_a * HEAD_DIM
    qb_w, kvb_w = h_b * HEAD_DIM, 2 * kv_b * HEAD_DIM
    qc_w, kvc_w = h_c * HEAD_DIM, 2 * kv_c * HEAD_DIM
    f_pad = MXU_COLS
    o_qa = 0
    o_kva = o_qa + qa_w
    o_qb = o_kva + kva_w
    o_kvb = o_qb + qb_w
    o_f = o_kvb + kvb_w
    o_qc = o_f + f_pad
    o_kvc = o_qc + qc_w
    n_u = o_kvc + kvc_w
    src_f = o_f
    src_qc = src_f + h_b
    src_g = src_qc + qc_w + kvc_w
    assert w_in.shape[2] == src_g + 3 * d_model

    w_u32 = jnp.concatenate(
        [w_in[:, :, :src_qc], jnp.zeros((depth, d_model, f_pad - h_b), w_in.dtype), w_in[:, :, src_qc:src_g]],
        axis=2)
    w_g32 = jnp.transpose(w_in[:, :, src_g:].reshape(depth, d_model, 3, d_model), (0, 2, 1, 3))
    w_u, w_g = w_u32.astype(BF16), w_g32.astype(BF16)
    w_a16, w_b16, w_c16, w_o16 = (w.astype(BF16) for w in (w_br_a, w_br_b, w_br_c, w_o))
    w_eg16, w_eu16, w_ed16 = (w.astype(BF16) for w in (w_exp_gate, w_exp_up, w_exp_down))
    w_r = jnp.concatenate([w_router_group, w_router_expert], axis=2)
    w_r = jnp.pad(w_r, ((0, 0), (0, 0), (0, LANES - w_r.shape[2])))
    r_hi = w_r.astype(BF16)
    r_res = w_r - r_hi.astype(F32)
    r_mid = r_res.astype(BF16)
    r_lo = (r_res - r_mid.astype(F32)).astype(BF16)
    w_r3 = jnp.stack([r_hi, r_mid, r_lo], axis=1)
    b_r = jnp.concatenate([b_router_group, b_router_expert], axis=1)
    b_r = jnp.pad(b_r, ((0, 0), (0, LANES - b_r.shape[1])))[:, None, :]
    b_f_pad = jnp.pad(b_f, ((0, 0), (0, LANES - h_b)))
    slopes = 2.0 ** (-ALIBI_MAX_BIAS * jnp.arange(1, h_c + 1, dtype=F32) / h_c)

    n_pool, page = cache_kv_a.shape[1], cache_kv_a.shape[2]
    pages_a, pages_b, pages_c = cache_pages(cache_kv_a), cache_pages(cache_kv_b), cache_pages(cache_kv_c)
    lf_pages = jnp.pad(jnp.transpose(cache_logf_b, (0, 1, 3, 2)),
                       ((0, 0), (0, 0), (0, _head_rows(h_b) - h_b), (0, 0)))

    n_p = batch * seq
    nq = seq // Q_TILE
    xp = x_prompt.reshape(n_p, d_model)
    xs = x_sample.reshape(dec_batch, d_model)
    rows_p = {k: [] for k in ("kv_a", "kv_b", "logf", "kv_c")}
    rows_s = {k: [] for k in ("kv_a", "kv_b", "logf", "kv_c")}

    yp = ys = None
    for l in range(depth):
        if yp is None:
            xn = rmsnorm(xp, g_norm_mix[l], BF16)
        else:
            xp, xn = add_rmsnorm(xp, yp, g_norm_mix[l], BF16, write_sum=True)
        u = matmul(xn, w_u, l, F32, name="in_proj")
        lf, cum = logf_cumsum(u, b_f_pad[l][None, :], batch, seq, o_f // LANES)
        c_rows = jnp.transpose(cum[:, :h_b].reshape(batch, seq, h_b), (0, 2, 1)).reshape(batch, h_b, nq, 1, Q_TILE)
        o_a = prompt_attention("a", u, batch, seq, h_a, kv_a, o_qa, o_kva)
        o_b = prompt_attention("b", u, batch, seq, h_b, kv_b, o_qb, o_kvb, c_rows)
        o_c = prompt_attention("c", u, batch, seq, h_c, kv_c, o_qc, o_kvc, slopes)
        merged = branch_merge(xn, o_a, o_b, o_c, w_g, w_a16, w_b16, w_c16, l)
        hp = matmul(merged, w_o16, l, F32, resid=xp, name="out_proj")
        rows_p["kv_a"].append(u[:, o_kva:o_kva + kva_w].reshape(batch, seq, 2, kv_a, HEAD_DIM))
        rows_p["kv_b"].append(u[:, o_kvb:o_kvb + kvb_w].reshape(batch, seq, 2, kv_b, HEAD_DIM))
        rows_p["kv_c"].append(u[:, o_kvc:o_kvc + kvc_w].reshape(batch, seq, 2, kv_c, HEAD_DIM))
        rows_p["logf"].append(lf[:, :h_b].reshape(batch, seq, h_b))
        yp = moe_sorted(hp, g_norm_ffn[l], w_r3, b_r, w_eg16, w_eu16, w_ed16, l, n_groups, e_per_group)
        xp = hp

        if ys is None:
            xn_s = rmsnorm(xs, g_norm_mix[l], F32)
        else:
            xs, xn_s = add_rmsnorm(xs, ys, g_norm_mix[l], F32, write_sum=True)
        us = matmul_hp(xn_s, w_u32, l, name="in_proj_s")
        lf_s = logf_only(us, b_f_pad[l][None, :], o_f // LANES)[:, :h_b]
        q_of = lambda off, h: us[:, off:off + h * HEAD_DIM].reshape(dec_batch, h, HEAD_DIM)
        kv_of = lambda off, n_kv, part: us[:, off + part * n_kv * HEAD_DIM: off + (part + 1) * n_kv * HEAD_DIM].reshape(
            dec_batch, n_kv, HEAD_DIM)
        o_as = decode_attention_a(q_of(o_qa, h_a), pages_a, l, page_table, kv_a)
        o_bs = decode_attention_b(q_of(o_qb, h_b), kv_of(o_kvb, kv_b, 0), kv_of(o_kvb, kv_b, 1), lf_s,
                                  pages_b, lf_pages, l, page_table, kv_b)
        o_cs = decode_attention_c(q_of(o_qc, h_c), kv_of(o_kvc, kv_c, 0), kv_of(o_kvc, kv_c, 1), slopes,
                                  pages_c, l, page_table, kv_c)
        merged_s = branch_merge_hp(xn_s, o_as, o_bs, o_cs, w_g32, w_br_a, w_br_b, w_br_c, l)
        hs = matmul_hp(merged_s, w_o, l, resid=xs, name="out_proj_s")
        rows_s["kv_a"].append(us[:, o_kva:o_kva + kva_w].reshape(dec_batch, 1, 2, kv_a, HEAD_DIM))
        rows_s["kv_b"].append(us[:, o_kvb:o_kvb + kvb_w].reshape(dec_batch, 1, 2, kv_b, HEAD_DIM))
        rows_s["kv_c"].append(us[:, o_kvc:o_kvc + kvc_w].reshape(dec_batch, 1, 2, kv_c, HEAD_DIM))
        rows_s["logf"].append(lf_s.reshape(dec_batch, 1, h_b))
        pk_s = moe_router(hs, g_norm_ffn[l], w_r3, b_r, l, n_groups, e_per_group)
        ys = moe_experts_hp(pk_s[:, :d_model], pk_s[:, d_model:], w_exp_gate, w_exp_up, w_exp_down, l, n_groups)
        xs = hs

    y_prompt = add_rmsnorm(xp, yp, g_norm_final, F32, write_sum=False).reshape(batch, seq, d_model)
    y_sample = add_rmsnorm(xs, ys, g_norm_final, F32, write_sum=False).reshape(dec_batch, 1, d_model)
    st = lambda xs_: jnp.stack(xs_, axis=0)
    return (y_prompt, y_sample, st(rows_p["kv_a"]), st(rows_p["kv_b"]), st(rows_p["logf"]), st(rows_p["kv_c"]),
            st(rows_s["kv_a"]), st(rows_s["kv_b"]), st(rows_s["logf"]), st(rows_s["kv_c"]))
```

```python
import functools

import jax
import jax.numpy as jnp
from jax import lax
from jax.experimental import pallas as pl
from jax.experimental.pallas import tpu as pltpu

F32 = jnp.float32
BF16 = jnp.bfloat16

HEAD_DIM = 128
LANES = 128
SUBLANES = 8
MXU_COLS = 256
VMEM_LIMIT_BIG = 56 * 1024 * 1024
VMEM_LIMIT_MID = 40 * 1024 * 1024
Q_TILE = 256
MOBA_BLOCK = 256
MOBA_TOPK = 3
ALIBI_MAX_BIAS = 8.0
TOPK_IN_GROUP = 2
RMS_EPS = 1e-6
NEG_BIG = -1e30
DEC_PAGES_PER_STEP = 16
MOE_TILE = 512
GATHER_ROWS = 128


def _cparams(sem, vmem=None):
    return pltpu.CompilerParams(dimension_semantics=sem, vmem_limit_bytes=vmem)


def _pick_tile(n, cap, mult):
    if n <= cap:
        return n
    best = None
    t = mult
    while t <= cap:
        if n % t == 0:
            best = t
        t += mult
    assert best is not None, (n, cap, mult)
    return best


def _split3(x):
    hi = x.astype(BF16)
    r = x - hi.astype(F32)
    mid = r.astype(BF16)
    lo = (r - mid.astype(F32)).astype(BF16)
    return hi, mid, lo


def _dot(a, b):
    return jnp.dot(a, b, preferred_element_type=F32)


def _dot_nt(a, b):
    return lax.dot_general(a, b, (((1,), (1,)), ((), ())), preferred_element_type=F32)


def _dot3(x, u):
    hi, mid, lo = _split3(x)
    return _dot(hi, u) + _dot(mid, u) + _dot(lo, u)


def _softplus(z):
    return jnp.maximum(z, 0.0) + jnp.log1p(jnp.exp(-jnp.abs(z)))


def _log_sigmoid(z):
    return jnp.minimum(z, 0.0) - jnp.log1p(jnp.exp(-jnp.abs(z)))


def _iota(shape, dim):
    return lax.broadcasted_iota(jnp.int32, shape, dim)


def _rowsum(x):
    return jnp.sum(x, axis=1, keepdims=True)


def _rowmax(x):
    return jnp.max(x, axis=1, keepdims=True)


def _rms_kernel(x_ref, g_ref, o_ref):
    x = x_ref[...]
    ms = jnp.mean(x * x, axis=-1, keepdims=True)
    o_ref[...] = ((x * lax.rsqrt(ms + RMS_EPS)) * g_ref[...]).astype(o_ref.dtype)


def rmsnorm(x, g, out_dtype):
    m, d = x.shape
    tm = _pick_tile(m, 256, 8)
    return pl.pallas_call(
        _rms_kernel,
        out_shape=jax.ShapeDtypeStruct((m, d), out_dtype),
        grid=(m // tm,),
        in_specs=[pl.BlockSpec((tm, d), lambda i: (i, 0)),
                  pl.BlockSpec((1, d), lambda i: (0, 0))],
        out_specs=pl.BlockSpec((tm, d), lambda i: (i, 0)),
        compiler_params=_cparams(("parallel",)),
        name="rmsnorm",
    )(x, g.reshape(1, d))


def _add_rms_kernel(a_ref, b_ref, g_ref, *out_refs):
    x = a_ref[...] + b_ref[...]
    ms = jnp.mean(x * x, axis=-1, keepdims=True)
    y = (x * lax.rsqrt(ms + RMS_EPS)) * g_ref[...]
    out_refs[-1][...] = y.astype(out_refs[-1].dtype)
    if len(out_refs) == 2:
        out_refs[0][...] = x


def add_rmsnorm(a, b, g, out_dtype, write_sum):
    m, d = a.shape
    tm = _pick_tile(m, 256, 8)
    row = pl.BlockSpec((tm, d), lambda i: (i, 0))
    out_shape = [jax.ShapeDtypeStruct((m, d), out_dtype)]
    out_specs = [row]
    if write_sum:
        out_shape.insert(0, jax.ShapeDtypeStruct((m, d), F32))
        out_specs.insert(0, row)
    res = pl.pallas_call(
        _add_rms_kernel,
        out_shape=tuple(out_shape),
        grid=(m // tm,),
        in_specs=[row, row, pl.BlockSpec((1, d), lambda i: (0, 0))],
        out_specs=tuple(out_specs),
        compiler_params=_cparams(("parallel",), VMEM_LIMIT_MID),
        name="add_rmsnorm",
    )(a, b, g.reshape(1, d))
    return res if write_sum else res[0]


def _mm_kernel(x_ref, w_ref, o_ref):
    o_ref[...] = _dot(x_ref[...], w_ref[...]).astype(o_ref.dtype)


def _mm_resid_kernel(x_ref, w_ref, r_ref, o_ref):
    o_ref[...] = (r_ref[...] + _dot(x_ref[...], w_ref[...])).astype(o_ref.dtype)


def matmul(x, w, layer, out_dtype, resid=None, name="matmul"):
    m, k = x.shape
    n = w.shape[2]
    tm = _pick_tile(m, 1024, 8)
    tn = _pick_tile(n, 1024, MXU_COLS)
    in_specs = [pl.BlockSpec((tm, k), lambda i, j: (i, 0)),
                pl.BlockSpec((None, k, tn), lambda i, j: (layer, 0, j))]
    args = [x, w]
    kern = _mm_kernel
    if resid is not None:
        in_specs.append(pl.BlockSpec((tm, tn), lambda i, j: (i, j)))
        args.append(resid)
        kern = _mm_resid_kernel
    return pl.pallas_call(
        kern,
        out_shape=jax.ShapeDtypeStruct((m, n), out_dtype),
        grid=(m // tm, n // tn),
        in_specs=in_specs,
        out_specs=pl.BlockSpec((tm, tn), lambda i, j: (i, j)),
        compiler_params=_cparams(("parallel", "parallel"), VMEM_LIMIT_BIG),
        name=name,
    )(*args)


def _logf_kernel(u_ref, b_ref, lf_ref):
    lf_ref[...] = _log_sigmoid(u_ref[...] + b_ref[...])


def logf_only(u, b_f_pad, col_block):
    m = u.shape[0]
    return pl.pallas_call(
        _logf_kernel,
        out_shape=jax.ShapeDtypeStruct((m, LANES), F32),
        grid=(1,),
        in_specs=[pl.BlockSpec((m, LANES), lambda i: (0, col_block)),
                  pl.BlockSpec((1, LANES), lambda i: (0, 0))],
        out_specs=pl.BlockSpec((m, LANES), lambda i: (0, 0)),
        name="logf_decode",
    )(u, b_f_pad)


def _logf_cum_kernel(u_ref, b_ref, lf_ref, c_ref, *, chunk):
    t = u_ref.shape[0]
    lf = _log_sigmoid(u_ref[...] + b_ref[...])
    lf_ref[...] = lf
    tri = (_iota((chunk, chunk), 0) >= _iota((chunk, chunk), 1)).astype(BF16)
    carry = jnp.zeros((1, LANES), F32)
    for n in range(t // chunk):
        hi, mid, lo = _split3(lf[n * chunk:(n + 1) * chunk])
        c = _dot(tri, hi) + _dot(tri, mid) + _dot(tri, lo) + carry
        c_ref[n * chunk:(n + 1) * chunk, :] = c
        carry = c[chunk - 1:chunk, :]


def logf_cumsum(u, b_f_pad, batch, seq, col_block):
    chunk = _pick_tile(seq, 256, 8)
    kern = functools.partial(_logf_cum_kernel, chunk=chunk)
    return pl.pallas_call(
        kern,
        out_shape=(jax.ShapeDtypeStruct((batch * seq, LANES), F32),
                   jax.ShapeDtypeStruct((batch * seq, LANES), F32)),
        grid=(batch,),
        in_specs=[pl.BlockSpec((seq, LANES), lambda b: (b, col_block)),
                  pl.BlockSpec((1, LANES), lambda b: (0, 0))],
        out_specs=(pl.BlockSpec((seq, LANES), lambda b: (b, 0)),
                   pl.BlockSpec((seq, LANES), lambda b: (b, 0))),
        compiler_params=_cparams(("parallel",)),
        name="logf_cumsum",
    )(u, b_f_pad)


def _colmax(x):
    return jnp.max(x, axis=0, keepdims=True)


def _colsum(x):
    return jnp.sum(x, axis=0, keepdims=True)


def _stage_kv(k_ref, v_ref, kb_sc, vt_sc, tq):
    for n in range(k_ref.shape[0] // tq):
        rows = slice(n * tq, (n + 1) * tq)
        kb_sc[n] = k_ref[rows, :].astype(BF16)
        vt_sc[n] = v_ref[rows, :].T.astype(BF16)


def _q_t(q_ref, g):
    return q_ref[:, g * HEAD_DIM:(g + 1) * HEAD_DIM].T.astype(BF16)


def _key_pos(tq):
    return _iota((tq, tq), 0), _iota((tq, tq), 1)


def _stack_heads(q, g_heads):
    return jnp.concatenate([q[:, g * HEAD_DIM:(g + 1) * HEAD_DIM] for g in range(g_heads)], axis=0)


def _unstack_heads(o, g_heads, tq):
    return jnp.concatenate([o[g * tq:(g + 1) * tq] for g in range(g_heads)], axis=1)


def _tile_pos(g_heads, tq):
    row_t = jnp.concatenate([_iota((tq, tq), 0)] * g_heads, axis=0)
    col = _iota((g_heads * tq, tq), 1)
    return row_t, col


def _attn_a_kernel(q_ref, k_ref, v_ref, o_ref, *, tq, g_heads, scale):
    i = pl.program_id(2)
    rows = g_heads * tq
    qs = _stack_heads(q_ref[...], g_heads).astype(BF16)
    row_t, col = _tile_pos(g_heads, tq)
    suffix = (_iota((tq, tq), 0) >= _iota((tq, tq), 1)).astype(BF16)

    def block(j, carry, masked):
        o, c = carry
        start = pl.multiple_of(j * tq, tq)
        k = k_ref[pl.ds(start, tq), :].astype(BF16)
        v = v_ref[pl.ds(start, tq), :].astype(BF16)
        z = _dot_nt(qs, k) * scale
        sp = _softplus(z)
        if masked:
            reads = col < row_t
            sp = jnp.where(reads, sp, 0.0)
        hi = sp.astype(BF16)
        lo = (sp - hi.astype(F32)).astype(BF16)
        s_incl = c + _dot(hi, suffix) + _dot(lo, suffix)
        w = jnp.exp(z - s_incl)
        if masked:
            w = jnp.where(reads, w, 0.0)
        o = o + _dot(w.astype(BF16), v)
        return o, s_incl[:, 0:1]

    carry = (jnp.zeros((rows, HEAD_DIM), F32), jnp.zeros((rows, 1), F32))
    carry = block(i, carry, True)
    carry = lax.fori_loop(0, i, lambda s, cr: block(i - 1 - s, cr, False), carry)
    o_ref[...] = _unstack_heads(carry[0], g_heads, tq).astype(o_ref.dtype)


def _attn_b_kernel(q_ref, k_ref, v_ref, c_ref, o_ref, *, tq, g_heads, scale):
    i = pl.program_id(2)
    rows = g_heads * tq
    qs = _stack_heads(q_ref[...], g_heads).astype(BF16)
    row_t, col = _tile_pos(g_heads, tq)
    eye = _iota((tq, tq), 0) == _iota((tq, tq), 1)
    cq = jnp.concatenate(
        [jnp.sum(jnp.where(eye, c_ref[g, i], 0.0), axis=1, keepdims=True) for g in range(g_heads)],
        axis=0)

    def block(j, carry, masked):
        m, l, acc = carry
        start = pl.multiple_of(j * tq, tq)
        k = k_ref[pl.ds(start, tq), :].astype(BF16)
        v = v_ref[pl.ds(start, tq), :].astype(BF16)
        z = _dot_nt(qs, k) * scale
        ck = jnp.concatenate([jnp.broadcast_to(c_ref[g, j], (tq, tq)) for g in range(g_heads)], axis=0)
        z = (z + cq) - ck
        if masked:
            z = jnp.where(col <= row_t, z, NEG_BIG)
        m_new = jnp.maximum(m, jnp.max(z, axis=1, keepdims=True))
        alpha = jnp.exp(m - m_new)
        p = jnp.exp(z - m_new)
        l = alpha * l + jnp.sum(p, axis=1, keepdims=True)
        acc = alpha * acc + _dot(p.astype(BF16), v)
        return m_new, l, acc

    carry = (jnp.full((rows, 1), NEG_BIG, F32), jnp.zeros((rows, 1), F32),
             jnp.zeros((rows, HEAD_DIM), F32))
    carry = block(i, carry, True)
    carry = lax.fori_loop(0, i, lambda s, cr: block(s, cr, False), carry)
    o = carry[2] / carry[1]
    o_ref[...] = _unstack_heads(o, g_heads, tq).astype(o_ref.dtype)


def _topk_rank_lanes(gate, lane, n_valid_lanes):
    rank = jnp.zeros(gate.shape, F32)
    for n in range(n_valid_lanes):
        gn = gate[:, n:n + 1]
        beats = (gate > gn) | ((gate == gn) & (lane < n))
        cnt = jnp.sum(beats.astype(F32), axis=1, keepdims=True)
        rank = jnp.where(lane == n, cnt, rank)
    return rank


def _attn_c_kernel(slopes_ref, q_ref, k_ref, v_ref, o_ref, kb_sc, vt_sc, km_sc, m_sc, l_sc, acc_sc,
                   *, tq, g_heads, scale, nb):
    h = pl.program_id(1)
    i = pl.program_id(2)
    nbp = km_sc.shape[0]

    @pl.when(i == 0)
    def _():
        _stage_kv(k_ref, v_ref, kb_sc, vt_sc, tq)
        km_sc[...] = jnp.zeros(km_sc.shape, F32)
        for n in range(nb):
            km_sc[n:n + 1, :] = jnp.sum(k_ref[n * tq:(n + 1) * tq, :], axis=0, keepdims=True) * (1.0 / tq)

    s_pos, t_pos = _key_pos(tq)
    rel = (t_pos - s_pos).astype(F32)
    blk = _iota((nbp, tq), 0)
    km = km_sc[...].astype(BF16)
    qts = [_q_t(q_ref, g) for g in range(g_heads)]
    slopes = [slopes_ref[h * g_heads + g] for g in range(g_heads)]
    sels = []
    for g in range(g_heads):
        gate = jnp.where(blk < i, _dot(km, qts[g]), -jnp.inf)
        sel_g = []
        for n in range(nb):
            gn = gate[n:n + 1, :]
            beats = (gate > gn) | ((gate == gn) & (blk < n))
            sel_g.append(_colsum(beats.astype(F32)) < float(MOBA_TOPK))
        sels.append(sel_g)
        zt = _dot(kb_sc[i], qts[g]) * scale - slopes[g] * rel
        zt = jnp.where(s_pos <= t_pos, zt, NEG_BIG)
        m0 = _colmax(zt)
        p = jnp.exp(zt - m0)
        m_sc[g] = m0
        l_sc[g] = _colsum(p)
        acc_sc[g] = _dot(vt_sc[i], p.astype(BF16))

    for n in range(nb):
        @pl.when(n < i)
        def _(n=n):
            kb = kb_sc[n]
            vt = vt_sc[n]
            dist = rel + ((i - n) * tq).astype(F32)
            for g in range(g_heads):
                take = sels[g][n]
                zt = jnp.where(take, _dot(kb, qts[g]) * scale - slopes[g] * dist, NEG_BIG)
                m_old = m_sc[g]
                m_new = jnp.maximum(m_old, _colmax(zt))
                alpha = jnp.exp(m_old - m_new)
                p = jnp.where(take, jnp.exp(zt - m_new), 0.0)
                m_sc[g] = m_new
                l_sc[g] = alpha * l_sc[g] + _colsum(p)
                acc_sc[g] = alpha * acc_sc[g] + _dot(vt, p.astype(BF16))

    for g in range(g_heads):
        o_ref[:, g * HEAD_DIM:(g + 1) * HEAD_DIM] = (acc_sc[g] / l_sc[g]).T.astype(o_ref.dtype)


def prompt_attention(kind, u, batch, seq, n_q_heads, n_kv, q_off, kv_off, extra=None):
    g_heads = n_q_heads // n_kv
    tq = Q_TILE
    assert seq % tq == 0 and q_off % (g_heads * HEAD_DIM) == 0 and kv_off % HEAD_DIM == 0
    nq = seq // tq
    qw = g_heads * HEAD_DIM
    scale = HEAD_DIM ** -0.5
    q_spec = pl.BlockSpec((tq, qw), lambda b, h, i, *_: (b * nq + i, q_off // qw + h))
    k_spec = pl.BlockSpec((seq, HEAD_DIM), lambda b, h, i, *_: (b, kv_off // HEAD_DIM + h))
    v_spec = pl.BlockSpec((seq, HEAD_DIM), lambda b, h, i, *_: (b, kv_off // HEAD_DIM + n_kv + h))
    o_spec = pl.BlockSpec((tq, qw), lambda b, h, i, *_: (b * nq + i, h))
    out_shape = jax.ShapeDtypeStruct((batch * seq, n_q_heads * HEAD_DIM), BF16)
    grid = (batch, n_kv, nq)
    sem = ("parallel", "parallel", "arbitrary")
    if kind == "a":
        kern = functools.partial(_attn_a_kernel, tq=tq, g_heads=g_heads, scale=scale)
        return pl.pallas_call(
            kern, out_shape=out_shape, grid=grid,
            in_specs=[q_spec, k_spec, v_spec], out_specs=o_spec,
            compiler_params=_cparams(sem, VMEM_LIMIT_MID), name="attn_stickbreak",
        )(u, u, u)
    if kind == "b":
        c_rows = extra
        c_spec = pl.BlockSpec((None, g_heads, nq, 1, tq), lambda b, h, i: (b, h, 0, 0, 0))
        kern = functools.partial(_attn_b_kernel, tq=tq, g_heads=g_heads, scale=scale)
        return pl.pallas_call(
            kern, out_shape=out_shape, grid=grid,
            in_specs=[q_spec, k_spec, v_spec, c_spec], out_specs=o_spec,
            compiler_params=_cparams(sem, VMEM_LIMIT_MID), name="attn_forget",
        )(u, u, u, c_rows)
    slopes = extra
    nb = seq // MOBA_BLOCK
    assert tq == MOBA_BLOCK
    nbp = -(-nb // SUBLANES) * SUBLANES
    staged = [pltpu.VMEM((nq, tq, HEAD_DIM), BF16), pltpu.VMEM((nq, HEAD_DIM, tq), BF16)]
    kern = functools.partial(_attn_c_kernel, tq=tq, g_heads=g_heads, scale=scale, nb=nb)
    return pl.pallas_call(
        kern, out_shape=out_shape,
        grid_spec=pltpu.PrefetchScalarGridSpec(
            num_scalar_prefetch=1, grid=grid,
            in_specs=[q_spec, k_spec, v_spec], out_specs=o_spec,
            scratch_shapes=staged + [pltpu.VMEM((nbp, HEAD_DIM), F32), pltpu.VMEM((g_heads, 1, tq), F32),
                                     pltpu.VMEM((g_heads, 1, tq), F32), pltpu.VMEM((g_heads, HEAD_DIM, tq), F32)]),
        compiler_params=_cparams(sem, VMEM_LIMIT_MID), name="attn_moba",
    )(slopes, u, u, u)


def _head_major(n_kv):
    return n_kv % 4 != 0


def cache_pages(cache):
    depth, n_pool, page, two, n_kv, d = cache.shape
    if _head_major(n_kv):
        cache = jnp.transpose(cache, (0, 1, 2, 4, 3, 5))
    return cache.reshape(depth, n_pool, page * two * n_kv, d)


def _page_kv(page_ref, h, n_kv, page, which):
    row = 2 * h + which if _head_major(n_kv) else which * n_kv + h
    return page_ref[pl.ds(row, page, stride=2 * n_kv), :]


def _lanes(x):
    return jnp.broadcast_to(x, (x.shape[0], LANES))


def _split2(x):
    hi = x.astype(BF16)
    return hi, (x - hi.astype(F32)).astype(BF16)


def _row_parts(x):
    hi, mid, lo = _split3(x)
    return jnp.concatenate([hi, mid, lo], axis=0), jnp.concatenate([hi, mid], axis=0)


def _fold(a, rows):
    out = a[:rows]
    for i in range(1, a.shape[0] // rows):
        out = out + a[i * rows:(i + 1) * rows]
    return out


def _mm_hp(x3, x2, w):
    rows = x2.shape[0] // 2
    wh, wm = _split2(w)
    return _fold(_dot(x3, wh), rows) + _fold(_dot(x2, wm), rows)


def _qk_hp(q3, q2, k):
    rows = q2.shape[0] // 2
    kh, km = _split2(k)
    return _fold(_dot_nt(q3, kh), rows) + _fold(_dot_nt(q2, km), rows)


def _pv_hp(p, v):
    rows = p.shape[0]
    ph, pm = _split2(p)
    vh, vm = _split2(v)
    return _fold(_dot(jnp.concatenate([ph, pm], axis=0), vh), rows) + _dot(ph, vm)


def _page_keys(page_ref, n_kv, page):
    return jnp.concatenate([_page_kv(page_ref, h, n_kv, page, 0) for h in range(n_kv)], axis=0)


def _page_values(page_ref, n_kv, page):
    return jnp.concatenate([_page_kv(page_ref, h, n_kv, page, 1) for h in range(n_kv)], axis=1)


def _own_head(x, n_kv, g_heads, width):
    row = _iota((x.shape[0], width), 0)
    out = jnp.zeros((x.shape[0], width), F32)
    for h in range(n_kv):
        mine = (row >= h * g_heads) & (row < (h + 1) * g_heads)
        out = out + jnp.where(mine, x[:, h * width:(h + 1) * width], 0.0)
    return out


def _scores(q3, q2, page_ref, n_kv, g_heads, page, scale):
    return _own_head(_qk_hp(q3, q2, _page_keys(page_ref, n_kv, page)), n_kv, g_heads, page) * scale


def _new_token_score(q, knew_ref, n_kv, g_heads, scale):
    row = _iota((q.shape[0], 1), 0)
    out = jnp.zeros((q.shape[0], 1), F32)
    for h in range(n_kv):
        mine = (row >= h * g_heads) & (row < (h + 1) * g_heads)
        out = out + jnp.where(mine, _rowsum(q * knew_ref[h]), 0.0)
    return out * scale


def _dec_a_kernel(pt_ref, q_ref, *rest, n_pages_step, n_kv, g_heads, page, scale):
    page_refs = rest[:n_pages_step]
    o_ref, acc_sc, c_sc = rest[n_pages_step:]
    j = pl.program_id(1)
    rows = q_ref.shape[0]

    @pl.when(j == 0)
    def _():
        acc_sc[...] = jnp.zeros(acc_sc.shape, F32)
        c_sc[...] = jnp.zeros(c_sc.shape, F32)

    q3, q2 = _row_parts(q_ref[...])
    suffix = (_iota((page, page), 0) >= _iota((page, page), 1)).astype(BF16)
    zs = [_scores(q3, q2, page_refs[r], n_kv, g_heads, page, scale) for r in range(n_pages_step)]
    sps = [_split2(_softplus(z)) for z in zs]
    stacked = jnp.concatenate([hi for hi, _ in sps] + [lo for _, lo in sps], axis=0)
    sums = _dot(stacked, suffix)
    c = c_sc[:, 0:1]
    acc = acc_sc[...]
    for r in range(n_pages_step):
        local = sums[r * rows:(r + 1) * rows] + sums[(n_pages_step + r) * rows:(n_pages_step + r + 1) * rows]
        s_incl = c + local
        acc = acc + _pv_hp(jnp.exp(zs[r] - s_incl), _page_values(page_refs[r], n_kv, page))
        c = s_incl[:, 0:1]
    acc_sc[...] = acc
    c_sc[...] = _lanes(c)

    @pl.when(j == pl.num_programs(1) - 1)
    def _():
        o_ref[...] = _own_head(acc, n_kv, g_heads, HEAD_DIM)


def _dec_b_kernel(pt_ref, q_ref, knew_ref, vnew_ref, lfnew_ref, *rest, n_pages_step, n_kv, g_heads, page, scale):
    page_refs = rest[:n_pages_step]
    lf_refs = rest[n_pages_step:2 * n_pages_step]
    o_ref, m_sc, l_sc, acc_sc, cb_sc = rest[2 * n_pages_step:]
    j = pl.program_id(1)
    q = q_ref[...]
    rows = q.shape[0]

    @pl.when(j == 0)
    def _():
        m_sc[...] = _lanes(_new_token_score(q, knew_ref, n_kv, g_heads, scale))
        l_sc[...] = jnp.ones(l_sc.shape, F32)
        acc_sc[...] = jnp.broadcast_to(vnew_ref[...], acc_sc.shape)
        cb_sc[...] = lfnew_ref[...]

    q3, q2 = _row_parts(q)
    after = (_iota((page, page), 0) > _iota((page, page), 1)).astype(BF16)
    lfs = [lf_refs[r][...] for r in range(n_pages_step)]
    later_all = _dot3(jnp.concatenate(lfs, axis=0), after)
    cb = cb_sc[:, 0:1]
    zs = []
    for r in range(n_pages_step):
        later = later_all[r * rows:(r + 1) * rows]
        z = _scores(q3, q2, page_refs[r], n_kv, g_heads, page, scale)
        zs.append(z + (cb + later))
        cb = cb + (later[:, 0:1] + lfs[r][:, 0:1])
    m_old = m_sc[:, 0:1]
    m_new = jnp.maximum(m_old, _rowmax(functools.reduce(jnp.maximum, zs)))
    alpha = jnp.exp(m_old - m_new)
    ps = [jnp.exp(z - m_new) for z in zs]
    l_new = alpha * l_sc[:, 0:1] + _rowsum(functools.reduce(jnp.add, ps))
    acc = alpha * acc_sc[...]
    for r in range(n_pages_step):
        acc = acc + _pv_hp(ps[r], _page_values(page_refs[r], n_kv, page))
    m_sc[...] = _lanes(m_new)
    l_sc[...] = _lanes(l_new)
    acc_sc[...] = acc
    cb_sc[...] = _lanes(cb)

    @pl.when(j == pl.num_programs(1) - 1)
    def _():
        o_ref[...] = _own_head(acc, n_kv, g_heads, HEAD_DIM) / l_new


def _dec_c_kernel(pt_ref, q_ref, knew_ref, vnew_ref, slope_ref, *rest,
                  n_blocks_step, n_pages_blk, n_kv, g_heads, page, scale, past_len, n_blocks):
    n_page_refs = n_blocks_step * n_pages_blk
    page_refs = rest[:n_page_refs]
    o_ref, part_sc, g_sc, m_sc, l_sc = rest[n_page_refs:]
    j = pl.program_id(1)
    q = q_ref[...]
    rows = q.shape[0]
    lane = _iota((rows, LANES), 1)
    n_keys = n_pages_blk * page

    @pl.when(j == 0)
    def _():
        g_sc[...] = jnp.zeros(g_sc.shape, F32)
        m_sc[...] = jnp.zeros(m_sc.shape, F32)
        l_sc[...] = jnp.zeros(l_sc.shape, F32)

    q3, q2 = _row_parts(q)
    slope = slope_ref[:, 0:1]
    row1 = _iota((rows, 1), 0)
    for bb in range(n_blocks_step):
        n = j * n_blocks_step + bb
        zs = []
        g_n = jnp.zeros((rows, 1), F32)
        refs = page_refs[bb * n_pages_blk:(bb + 1) * n_pages_blk]
        for r, page_ref in enumerate(refs):
            pos = n * n_keys + r * page + _iota((1, page), 1)
            dist = (past_len - pos).astype(F32)
            zs.append(_scores(q3, q2, page_ref, n_kv, g_heads, page, scale) - slope * dist)
        for h in range(n_kv):
            ksum = functools.reduce(jnp.add, [jnp.sum(_page_kv(pr, h, n_kv, page, 0), axis=0, keepdims=True)
                                              for pr in refs])
            mine = (row1 >= h * g_heads) & (row1 < (h + 1) * g_heads)
            g_n = g_n + jnp.where(mine, _rowsum(q * (ksum * (1.0 / n_keys))), 0.0)
        m_n = _rowmax(functools.reduce(jnp.maximum, zs))
        ps = [jnp.exp(z - m_n) for z in zs]
        l_n = _rowsum(functools.reduce(jnp.add, ps))
        part_sc[n] = functools.reduce(
            jnp.add, [_pv_hp(p, _page_values(pr, n_kv, page)) for p, pr in zip(ps, refs)])
        g_sc[...] = jnp.where(lane == n, g_n, g_sc[...])
        m_sc[...] = jnp.where(lane == n, m_n, m_sc[...])
        l_sc[...] = jnp.where(lane == n, l_n, l_sc[...])

    @pl.when(j == pl.num_programs(1) - 1)
    def _():
        gate = jnp.where(lane < n_blocks, g_sc[...], -jnp.inf)
        rank = _topk_rank_lanes(gate, lane, n_blocks)
        sel = (rank < float(MOBA_TOPK)) & (lane < n_blocks)
        z_own = _new_token_score(q, knew_ref, n_kv, g_heads, scale)
        m_star = jnp.maximum(z_own, _rowmax(jnp.where(sel, m_sc[...], NEG_BIG)))
        wts = jnp.where(sel, jnp.exp(m_sc[...] - m_star), 0.0)
        w_own = jnp.exp(z_own - m_star)
        l_tot = w_own + _rowsum(wts * l_sc[...])
        o = w_own * jnp.broadcast_to(vnew_ref[...], (rows, vnew_ref.shape[1]))
        for n in range(n_blocks):
            o = o + wts[:, n:n + 1] * part_sc[n]
        o_ref[...] = _own_head(o, n_kv, g_heads, HEAD_DIM) / l_tot


def _head_rows(n_q_heads):
    return -(-n_q_heads // SUBLANES) * SUBLANES


def _pad_heads(x):
    rows = _head_rows(x.shape[1])
    return jnp.pad(x, ((0, 0), (0, rows - x.shape[1])) + ((0, 0),) * (x.ndim - 2))


def _decode_call(kern, name, page_table, n_steps, rows, per_seq_inputs, shared_inputs, paged, scratch):
    s = page_table.shape[0]
    in_specs = [pl.BlockSpec((None,) + x.shape[1:], lambda b, j, pt, nd=x.ndim: (b,) + (0,) * (nd - 1))
                for x in per_seq_inputs]
    in_specs += [pl.BlockSpec(x.shape, lambda b, j, pt, nd=x.ndim: (0,) * nd) for x in shared_inputs]
    args = list(per_seq_inputs) + list(shared_inputs)
    for arr, layer, page_fn in paged:
        in_specs.append(pl.BlockSpec(
            (None, None) + arr.shape[2:],
            lambda b, j, pt, layer=layer, page_fn=page_fn: (layer, page_fn(b, j, pt), 0, 0)))
        args.append(arr)
    return pl.pallas_call(
        kern,
        out_shape=jax.ShapeDtypeStruct((s, rows, HEAD_DIM), F32),
        grid_spec=pltpu.PrefetchScalarGridSpec(
            num_scalar_prefetch=1, grid=(s, n_steps),
            in_specs=in_specs,
            out_specs=pl.BlockSpec((None, rows, HEAD_DIM), lambda b, j, pt: (b, 0, 0)),
            scratch_shapes=scratch),
        compiler_params=_cparams(("parallel", "arbitrary"), VMEM_LIMIT_BIG),
        name=name,
    )(page_table, *args)


def _state(rows):
    return pltpu.VMEM((rows, LANES), F32)


def decode_attention_a(q, pages, layer, page_table, n_kv):
    s, h, _ = q.shape
    rows = _head_rows(h)
    n_pages = page_table.shape[1]
    page = pages.shape[2] // (2 * n_kv)
    pps = _pick_tile(n_pages, DEC_PAGES_PER_STEP, 1)
    latest_first = lambda r: (lambda b, j, pt: pt[b, n_pages - 1 - (j * pps + r)])
    kern = functools.partial(_dec_a_kernel, n_pages_step=pps, n_kv=n_kv, g_heads=h // n_kv, page=page,
                             scale=HEAD_DIM ** -0.5)
    out = _decode_call(kern, "decode_stickbreak", page_table, n_pages // pps, rows,
                       [_pad_heads(q)], [],
                       [(pages, layer, latest_first(r)) for r in range(pps)],
                       [pltpu.VMEM((rows, n_kv * HEAD_DIM), F32), _state(rows)])
    return out[:, :h].reshape(s, h * HEAD_DIM)


def decode_attention_b(q, k_new, v_new, lf_new, pages, lf_pages, layer, page_table, n_kv):
    s, h, _ = q.shape
    rows = _head_rows(h)
    n_pages = page_table.shape[1]
    page = pages.shape[2] // (2 * n_kv)
    pps = _pick_tile(n_pages, DEC_PAGES_PER_STEP, 1)
    latest_first = lambda r: (lambda b, j, pt: pt[b, n_pages - 1 - (j * pps + r)])
    lfn = jnp.broadcast_to(_pad_heads(lf_new)[..., None], (s, rows, LANES))
    kern = functools.partial(_dec_b_kernel, n_pages_step=pps, n_kv=n_kv, g_heads=h // n_kv, page=page,
                             scale=HEAD_DIM ** -0.5)
    out = _decode_call(kern, "decode_forget", page_table, n_pages // pps, rows,
                       [_pad_heads(q), k_new[:, :, None, :], v_new.reshape(s, 1, n_kv * HEAD_DIM), lfn], [],
                       [(pages, layer, latest_first(r)) for r in range(pps)]
                       + [(lf_pages, layer, latest_first(r)) for r in range(pps)],
                       [_state(rows), _state(rows), pltpu.VMEM((rows, n_kv * HEAD_DIM), F32), _state(rows)])
    return out[:, :h].reshape(s, h * HEAD_DIM)


def decode_attention_c(q, k_new, v_new, slopes, pages, layer, page_table, n_kv):
    s, h, _ = q.shape
    rows = _head_rows(h)
    n_pages = page_table.shape[1]
    page = pages.shape[2] // (2 * n_kv)
    assert MOBA_BLOCK % page == 0
    ppb = MOBA_BLOCK // page
    assert n_pages % ppb == 0
    n_blocks = n_pages // ppb
    assert n_blocks <= LANES
    bps = _pick_tile(n_blocks, max(DEC_PAGES_PER_STEP // ppb, 1), 1)
    pps = bps * ppb
    slope_rows = jnp.broadcast_to(_pad_heads(slopes[None])[0][:, None], (rows, LANES))
    in_order = lambda r: (lambda b, j, pt: pt[b, j * pps + r])
    kern = functools.partial(_dec_c_kernel, n_blocks_step=bps, n_pages_blk=ppb, n_kv=n_kv, g_heads=h // n_kv,
                             page=page, scale=HEAD_DIM ** -0.5, past_len=n_pages * page, n_blocks=n_blocks)
    out = _decode_call(kern, "decode_moba", page_table, n_blocks // bps, rows,
                       [_pad_heads(q), k_new[:, :, None, :], v_new.reshape(s, 1, n_kv * HEAD_DIM)], [slope_rows],
                       [(pages, layer, in_order(r)) for r in range(pps)],
                       [pltpu.VMEM((n_blocks, rows, n_kv * HEAD_DIM), F32)] + [_state(rows)] * 3)
    return out[:, :h].reshape(s, h * HEAD_DIM)


def _mm_hp_kernel(x_ref, w_ref, *rest):
    x3, x2 = _row_parts(x_ref[...])
    out = _mm_hp(x3, x2, w_ref[...])
    if len(rest) == 2:
        out = out + rest[0][...]
    rest[-1][...] = out


def matmul_hp(x, w, layer, resid=None, name="matmul_hp"):
    m, k = x.shape
    n = w.shape[2]
    tn = _pick_tile(n, 512, MXU_COLS)
    in_specs = [pl.BlockSpec((m, k), lambda j: (0, 0)),
                pl.BlockSpec((None, k, tn), lambda j: (layer, 0, j))]
    args = [x, w]
    if resid is not None:
        in_specs.append(pl.BlockSpec((m, tn), lambda j: (0, j)))
        args.append(resid)
    return pl.pallas_call(
        _mm_hp_kernel,
        out_shape=jax.ShapeDtypeStruct((m, n), F32),
        grid=(n // tn,),
        in_specs=in_specs,
        out_specs=pl.BlockSpec((m, tn), lambda j: (0, j)),
        compiler_params=_cparams(("parallel",), VMEM_LIMIT_BIG),
        name=name,
    )(*args)


def _merge_hp_kernel(xn_ref, oa_ref, ob_ref, oc_ref, wg_ref, wa_ref, wb_ref, wc_ref, o_ref):
    x3, x2 = _row_parts(xn_ref[...])
    acc = None
    for br, (o_r, w_r) in enumerate(((oa_ref, wa_ref), (ob_ref, wb_ref), (oc_ref, wc_ref))):
        gate = jax.nn.sigmoid(_mm_hp(x3, x2, wg_ref[br]))
        o3, o2 = _row_parts(o_r[...])
        term = gate * _mm_hp(o3, o2, w_r[...])
        acc = term if acc is None else acc + term
    o_ref[...] = acc


def branch_merge_hp(xn, o_a, o_b, o_c, wg, w_a, w_b, w_c, layer):
    m, d = xn.shape
    tn = _pick_tile(d, 256, LANES)
    act = lambda a: pl.BlockSpec(a.shape, lambda j: (0, 0))
    wsp = lambda w: pl.BlockSpec((None, w.shape[1], tn), lambda j: (layer, 0, j))
    return pl.pallas_call(
        _merge_hp_kernel,
        out_shape=jax.ShapeDtypeStruct((m, d), F32),
        grid=(d // tn,),
        in_specs=[act(xn), act(o_a), act(o_b), act(o_c),
                  pl.BlockSpec((None, 3, d, tn), lambda j: (layer, 0, 0, j)),
                  wsp(w_a), wsp(w_b), wsp(w_c)],
        out_specs=pl.BlockSpec((m, tn), lambda j: (0, j)),
        compiler_params=_cparams(("parallel",), VMEM_LIMIT_BIG),
        name="branch_merge_hp",
    )(xn, o_a, o_b, o_c, wg, w_a, w_b, w_c)


def _experts_hp_kernel(eid_ref, ok_ref, x_ref, comb_ref, wg_ref, wu_ref, wd_ref, o_ref, *, n_groups):
    s = pl.program_id(0)
    f = pl.program_id(1)

    @pl.when((s == 0) & (f == 0))
    def _():
        o_ref[...] = jnp.zeros(o_ref.shape, F32)

    @pl.when(ok_ref[s] > 0)
    def _():
        x3, x2 = _row_parts(x_ref[...])
        lane = _iota(comb_ref.shape, 1)
        w_tok = _rowsum(jnp.where(lane == n_groups + eid_ref[s], comb_ref[...], 0.0))
        hg = _mm_hp(x3, x2, wg_ref[...])
        hu = _mm_hp(x3, x2, wu_ref[...])
        hid = (hg * jax.nn.sigmoid(hg)) * hu * w_tok
        h3, h2 = _row_parts(hid)
        o_ref[...] += _mm_hp(h3, h2, wd_ref[...])


def moe_experts_hp(xn, comb, w_eg, w_eu, w_ed, layer, n_groups):
    m, d = xn.shape
    n_experts, d_ff = w_eg.shape[1], w_eg.shape[3]
    tf = _pick_tile(d_ff, 256, LANES)
    nf = d_ff // tf
    active = jnp.any(comb[:, n_groups:n_groups + n_experts] > 0.0, axis=0)
    order = jnp.argsort(jnp.logical_not(active), stable=True).astype(jnp.int32)
    n_active = jnp.sum(active.astype(jnp.int32))
    step = jnp.arange(n_experts, dtype=jnp.int32)
    ok = (step < n_active).astype(jnp.int32)
    eid = order[jnp.minimum(step, jnp.maximum(n_active - 1, 0))]

    def fidx(s, f, eid_ref, ok_ref):
        return jnp.where(ok_ref[s] > 0, f, nf - 1)

    kern = functools.partial(_experts_hp_kernel, n_groups=n_groups)
    return pl.pallas_call(
        kern,
        out_shape=jax.ShapeDtypeStruct((m, d), F32),
        grid_spec=pltpu.PrefetchScalarGridSpec(
            num_scalar_prefetch=2, grid=(n_experts, nf),
            in_specs=[pl.BlockSpec((m, d), lambda s, f, e, k: (0, 0)),
                      pl.BlockSpec((m, LANES), lambda s, f, e, k: (0, 0)),
                      pl.BlockSpec((None, None, d, tf), lambda s, f, e, k: (layer, e[s], 0, fidx(s, f, e, k))),
                      pl.BlockSpec((None, None, d, tf), lambda s, f, e, k: (layer, e[s], 0, fidx(s, f, e, k))),
                      pl.BlockSpec((None, None, tf, d), lambda s, f, e, k: (layer, e[s], fidx(s, f, e, k), 0))],
            out_specs=pl.BlockSpec((m, d), lambda s, f, e, k: (0, 0))),
        compiler_params=_cparams(("arbitrary", "arbitrary"), VMEM_LIMIT_BIG),
        name="moe_experts_hp",
    )(eid, ok, xn, comb, w_eg, w_eu, w_ed)


def _merge_kernel(xn_ref, oa_ref, ob_ref, oc_ref, wg_ref, wa_ref, wb_ref, wc_ref, o_ref):
    xn = xn_ref[...]
    acc = None
    for br, (o_r, w_r) in enumerate(((oa_ref, wa_ref), (ob_ref, wb_ref), (oc_ref, wc_ref))):
        gate = jax.nn.sigmoid(_dot(xn, wg_ref[br]))
        term = gate * _dot(o_r[...], w_r[...])
        acc = term if acc is None else acc + term
    o_ref[...] = acc.astype(o_ref.dtype)


def branch_merge(xn, o_a, o_b, o_c, wg, w_a, w_b, w_c, layer):
    m, d = xn.shape
    tm = _pick_tile(m, 512, 8)
    tn = _pick_tile(d, 256, LANES)
    act = lambda a: pl.BlockSpec((tm, a.shape[1]), lambda i, j: (i, 0))
    wsp = lambda w: pl.BlockSpec((None, w.shape[1], tn), lambda i, j: (layer, 0, j))
    return pl.pallas_call(
        _merge_kernel,
        out_shape=jax.ShapeDtypeStruct((m, d), BF16),
        grid=(m // tm, d // tn),
        in_specs=[act(xn), act(o_a), act(o_b), act(o_c),
                  pl.BlockSpec((None, 3, d, tn), lambda i, j: (layer, 0, 0, j)),
                  wsp(w_a), wsp(w_b), wsp(w_c)],
        out_specs=pl.BlockSpec((tm, tn), lambda i, j: (i, j)),
        compiler_params=_cparams(("parallel", "parallel"), VMEM_LIMIT_BIG),
        name="branch_merge",
    )(xn, o_a, o_b, o_c, wg, w_a, w_b, w_c)


def _router_kernel(x_ref, g_ref, w_ref, b_ref, o_ref, *, n_groups, e_per_group):
    x = x_ref[...]
    d = x.shape[1]
    ms = jnp.mean(x * x, axis=-1, keepdims=True)
    xn = (x * lax.rsqrt(ms + RMS_EPS)) * g_ref[...]
    xh, xm, xl = _split3(xn)
    wh, wm, wl = _split3(w_ref[...])
    logits = (_dot(xh, wh) + (_dot(xh, wm) + _dot(xm, wh))
              + (_dot(xh, wl) + _dot(xm, wm) + _dot(xl, wh))) + b_ref[...]
    rows = logits.shape[0]
    lane = _iota((rows, LANES), 1).astype(F32)
    n_experts = n_groups * e_per_group
    is_g = lane < n_groups
    lg = jnp.where(is_g, logits, -jnp.inf)
    mg = jnp.max(lg, axis=1, keepdims=True)
    grp = jnp.min(jnp.where(is_g & (lg == mg), lane, float(LANES)), axis=1, keepdims=True)
    p_grp = 1.0 / jnp.sum(jnp.exp(lg - mg), axis=1, keepdims=True)
    e_lo = n_groups + grp * e_per_group
    in_grp = (lane >= e_lo) & (lane < e_lo + e_per_group)
    le = jnp.where(in_grp, logits, -jnp.inf)
    me = jnp.max(le, axis=1, keepdims=True)
    ee = jnp.exp(le - me)
    pe = ee / jnp.sum(ee, axis=1, keepdims=True)
    rank = jnp.zeros((rows, LANES), F32)
    for e in range(n_experts):
        ln = n_groups + e
        pv = pe[:, ln:ln + 1]
        beats = in_grp & ((pe > pv) | ((pe == pv) & (lane < ln)))
        cnt = jnp.sum(beats.astype(F32), axis=1, keepdims=True)
        rank = jnp.where(lane == ln, cnt, rank)
    top = in_grp & (rank < float(TOPK_IN_GROUP))
    top_p = jnp.where(top, pe, 0.0)
    top_p = top_p / jnp.sum(top_p, axis=1, keepdims=True)
    o_ref[:, :d] = xn
    o_ref[:, d:] = jnp.where(lane == n_groups + n_experts, grp, p_grp * top_p)


def moe_router(h, g, w_r, b_r, layer, n_groups, e_per_group):
    m, d = h.shape
    tm = _pick_tile(m, 256, 8)
    kern = functools.partial(_router_kernel, n_groups=n_groups, e_per_group=e_per_group)
    return pl.pallas_call(
        kern,
        out_shape=jax.ShapeDtypeStruct((m, d + LANES), F32),
        grid=(m // tm,),
        in_specs=[pl.BlockSpec((tm, d), lambda i: (i, 0)),
                  pl.BlockSpec((1, d), lambda i: (0, 0)),
                  pl.BlockSpec((None, d, LANES), lambda i: (layer, 0, 0)),
                  pl.BlockSpec((None, 1, LANES), lambda i: (layer, 0, 0))],
        out_specs=pl.BlockSpec((tm, d + LANES), lambda i: (i, 0)),
        compiler_params=_cparams(("parallel",), VMEM_LIMIT_MID),
        name="moe_router",
    )(h, g.reshape(1, d), w_r, b_r)


def _gather_kernel(idx_ref, src_hbm, *rest, rows, splits):
    out_refs = rest[:len(splits)]
    buf, sem = rest[len(splits):]
    i = pl.program_id(0)
    slot = lax.rem(i, 2)

    def row_copy(r, src_row, slot):
        return pltpu.make_async_copy(src_hbm.at[pl.ds(src_row, 1), :], buf.at[slot, pl.ds(r, 1), :], sem.at[slot])

    def start_tile(tile, slot):
        for r in range(rows):
            row_copy(r, idx_ref[tile * rows + r], slot).start()

    @pl.when(i == 0)
    def _():
        start_tile(0, 0)

    @pl.when(i + 1 < pl.num_programs(0))
    def _():
        start_tile(i + 1, 1 - slot)

    for r in range(rows):
        row_copy(r, 0, slot).wait()
    for o_ref, (lo, hi) in zip(out_refs, splits):
        o_ref[...] = buf[slot, :, lo:hi].astype(o_ref.dtype)


def gather_rows(src, idx, splits, dtypes):
    n_out = idx.shape[0]
    w = src.shape[1]
    rows = GATHER_ROWS
    assert n_out % rows == 0
    kern = functools.partial(_gather_kernel, rows=rows, splits=tuple(splits))
    res = pl.pallas_call(
        kern,
        out_shape=tuple(jax.ShapeDtypeStruct((n_out, hi - lo), dt) for (lo, hi), dt in zip(splits, dtypes)),
        grid_spec=pltpu.PrefetchScalarGridSpec(
            num_scalar_prefetch=1, grid=(n_out // rows,),
            in_specs=[pl.BlockSpec(memory_space=pl.ANY)],
            out_specs=tuple(pl.BlockSpec((rows, hi - lo), lambda i, idx_ref: (i, 0)) for lo, hi in splits),
            scratch_shapes=[pltpu.VMEM((2, rows, w), F32), pltpu.SemaphoreType.DMA((2,))]),
        compiler_params=_cparams(("arbitrary",)),
        name="gather_rows",
    )(idx, src)
    return res


def _group_experts_kernel(tg_ref, tv_ref, x_ref, comb_ref, wg_ref, wu_ref, wd_ref, o_ref,
                          *, n_groups, e_per_group):
    i = pl.program_id(0)
    e = pl.program_id(1)
    f = pl.program_id(2)

    @pl.when((e == 0) & (f == 0))
    def _():
        o_ref[...] = jnp.zeros(o_ref.shape, F32)

    @pl.when(tv_ref[i] > 0)
    def _():
        x = x_ref[...]
        lane = _iota(comb_ref.shape, 1)
        e_lane = n_groups + tg_ref[i] * e_per_group + e
        w_tok = jnp.sum(jnp.where(lane == e_lane, comb_ref[...], 0.0), axis=1, keepdims=True)
        hg = _dot(x, wg_ref[...])
        hu = _dot(x, wu_ref[...])
        hid = (hg * jax.nn.sigmoid(hg)) * hu * w_tok
        o_ref[...] += _dot(hid.astype(BF16), wd_ref[...])


def moe_group_experts(xs, combs, tile_group, tile_valid, w_eg, w_eu, w_ed, layer, n_groups, e_per_group):
    m, d = xs.shape
    d_ff = w_eg.shape[3]
    tm = MOE_TILE
    tf = _pick_tile(d_ff, 512, LANES)
    nf = d_ff // tf

    def w_idx(i, e, f, tg, tv):
        ok = tv[i] > 0
        return tg[i] * e_per_group + jnp.where(ok, e, e_per_group - 1), jnp.where(ok, f, nf - 1)

    def wcol(i, e, f, tg, tv):
        ex, fx = w_idx(i, e, f, tg, tv)
        return (layer, ex, 0, fx)

    def wrow(i, e, f, tg, tv):
        ex, fx = w_idx(i, e, f, tg, tv)
        return (layer, ex, fx, 0)

    kern = functools.partial(_group_experts_kernel, n_groups=n_groups, e_per_group=e_per_group)
    return pl.pallas_call(
        kern,
        out_shape=jax.ShapeDtypeStruct((m, d), F32),
        grid_spec=pltpu.PrefetchScalarGridSpec(
            num_scalar_prefetch=2, grid=(m // tm, e_per_group, nf),
            in_specs=[pl.BlockSpec((tm, d), lambda i, e, f, tg, tv: (i, 0)),
                      pl.BlockSpec((tm, LANES), lambda i, e, f, tg, tv: (i, 0)),
                      pl.BlockSpec((None, None, d, tf), wcol),
                      pl.BlockSpec((None, None, d, tf), wcol),
                      pl.BlockSpec((None, None, tf, d), wrow)],
            out_specs=pl.BlockSpec((tm, d), lambda i, e, f, tg, tv: (i, 0))),
        compiler_params=_cparams(("parallel", "arbitrary", "arbitrary"), VMEM_LIMIT_BIG),
        name="moe_group_experts",
    )(tile_group, tile_valid, xs, combs, w_eg, w_eu, w_ed)


def moe_sorted(h, g, w_r, b_r, w_eg, w_eu, w_ed, layer, n_groups, e_per_group):
    m, d = h.shape
    tm = MOE_TILE
    n_experts = n_groups * e_per_group
    packed = moe_router(h, g, w_r, b_r, layer, n_groups, e_per_group)
    grp = packed[:, d + n_groups + n_experts].astype(jnp.int32)
    onehot = (grp[:, None] == jnp.arange(n_groups)[None, :]).astype(jnp.int32)
    counts = jnp.sum(onehot, axis=0)
    rank = jnp.sum((jnp.cumsum(onehot, axis=0) - onehot) * onehot, axis=1)
    padded = ((counts + tm - 1) // tm) * tm
    ends = jnp.cumsum(padded)
    starts = ends - padded
    pos = starts[grp] + rank
    cap = m + n_groups * tm
    perm = jnp.zeros((cap,), jnp.int32).at[pos].set(jnp.arange(m, dtype=jnp.int32))
    tile_start = jnp.arange(cap // tm, dtype=jnp.int32) * tm
    tile_group = jnp.minimum(jnp.searchsorted(ends, tile_start, side="right"), n_groups - 1).astype(jnp.int32)
    tile_valid = (tile_start < ends[-1]).astype(jnp.int32)
    xs, combs = gather_rows(packed, perm, [(0, d), (d, d + LANES)], [BF16, F32])
    ys = moe_group_experts(xs, combs, tile_group, tile_valid, w_eg, w_eu, w_ed, layer, n_groups, e_per_group)
    (y,) = gather_rows(ys, pos, [(0, d)], [F32])
    return y


def kernel(x_prompt, x_sample, cache_kv_a, cache_kv_b, cache_logf_b, cache_kv_c, page_table, w_in, b_f, w_br_a, w_br_b, w_br_c, w_o, g_norm_mix, g_norm_ffn, w_router_group, b_router_group, w_router_expert, b_router_expert, w_exp_gate, w_exp_up, w_exp_down, g_norm_final):
    batch, seq, d_model = x_prompt.shape
    dec_batch, dec_seq, _ = x_sample.shape
    assert dec_seq == 1
    depth = w_in.shape[0]
    kv_a, kv_b, kv_c = cache_kv_a.shape[4], cache_kv_b.shape[4], cache_kv_c.shape[4]
    h_a, h_b, h_c = (w.shape[1] // HEAD_DIM for w in (w_br_a, w_br_b, w_br_c))
    assert cache_logf_b.shape[3] == h_b
    n_groups = w_router_group.shape[2]
    n_experts = w_router_expert.shape[2]
    e_per_group = n_experts // n_groups
    assert n_groups + n_experts < LANES

    qa_w, kva_w = h_a * HEAD_DIM, 2 * kv_a * HEAD_DIM
    qb_w, kvb_w = h_b * HEAD_DIM, 2 * kv_b * HEAD_DIM
    qc_w, kvc_w = h_c * HEAD_DIM, 2 * kv_c * HEAD_DIM
    f_pad = MXU_COLS
    o_qa = 0
    o_kva = o_qa + qa_w
    o_qb = o_kva + kva_w
    o_kvb = o_qb + qb_w
    o_f = o_kvb + kvb_w
    o_qc = o_f + f_pad
    o_kvc = o_qc + qc_w
    n_u = o_kvc + kvc_w
    src_f = o_f
    src_qc = src_f + h_b
    src_g = src_qc + qc_w + kvc_w
    assert w_in.shape[2] == src_g + 3 * d_model

    w_u32 = jnp.concatenate(
        [w_in[:, :, :src_qc], jnp.zeros((depth, d_model, f_pad - h_b), w_in.dtype), w_in[:, :, src_qc:src_g]],
        axis=2)
    w_g32 = jnp.transpose(w_in[:, :, src_g:].reshape(depth, d_model, 3, d_model), (0, 2, 1, 3))
    w_u, w_g = w_u32.astype(BF16), w_g32.astype(BF16)
    w_a16, w_b16, w_c16, w_o16 = (w.astype(BF16) for w in (w_br_a, w_br_b, w_br_c, w_o))
    w_eg16, w_eu16, w_ed16 = (w.astype(BF16) for w in (w_exp_gate, w_exp_up, w_exp_down))
    w_r = jnp.concatenate([w_router_group, w_router_expert], axis=2)
    w_r = jnp.pad(w_r, ((0, 0), (0, 0), (0, LANES - w_r.shape[2])))
    b_r = jnp.concatenate([b_router_group, b_router_expert], axis=1)
    b_r = jnp.pad(b_r, ((0, 0), (0, LANES - b_r.shape[1])))[:, None, :]
    b_f_pad = jnp.pad(b_f, ((0, 0), (0, LANES - h_b)))
    slopes = 2.0 ** (-ALIBI_MAX_BIAS * jnp.arange(1, h_c + 1, dtype=F32) / h_c)

    n_pool, page = cache_kv_a.shape[1], cache_kv_a.shape[2]
    pages_a, pages_b, pages_c = cache_pages(cache_kv_a), cache_pages(cache_kv_b), cache_pages(cache_kv_c)
    lf_pages = jnp.pad(jnp.transpose(cache_logf_b, (0, 1, 3, 2)),
                       ((0, 0), (0, 0), (0, _head_rows(h_b) - h_b), (0, 0)))

    n_p = batch * seq
    nq = seq // Q_TILE
    xp = x_prompt.reshape(n_p, d_model)
    xs = x_sample.reshape(dec_batch, d_model)
    rows_p = {k: [] for k in ("kv_a", "kv_b", "logf", "kv_c")}
    rows_s = {k: [] for k in ("kv_a", "kv_b", "logf", "kv_c")}

    yp = ys = None
    for l in range(depth):
        if yp is None:
            xn = rmsnorm(xp, g_norm_mix[l], BF16)
        else:
            xp, xn = add_rmsnorm(xp, yp, g_norm_mix[l], BF16, write_sum=True)
        u = matmul(xn, w_u, l, F32, name="in_proj")
        lf, cum = logf_cumsum(u, b_f_pad[l][None, :], batch, seq, o_f // LANES)
        c_rows = jnp.transpose(cum[:, :h_b].reshape(batch, seq, h_b), (0, 2, 1)).reshape(batch, h_b, nq, 1, Q_TILE)
        o_a = prompt_attention("a", u, batch, seq, h_a, kv_a, o_qa, o_kva)
        o_b = prompt_attention("b", u, batch, seq, h_b, kv_b, o_qb, o_kvb, c_rows)
        o_c = prompt_attention("c", u, batch, seq, h_c, kv_c, o_qc, o_kvc, slopes)
        merged = branch_merge(xn, o_a, o_b, o_c, w_g, w_a16, w_b16, w_c16, l)
        hp = matmul(merged, w_o16, l, F32, resid=xp, name="out_proj")
        rows_p["kv_a"].append(u[:, o_kva:o_kva + kva_w].reshape(batch, seq, 2, kv_a, HEAD_DIM))
        rows_p["kv_b"].append(u[:, o_kvb:o_kvb + kvb_w].reshape(batch, seq, 2, kv_b, HEAD_DIM))
        rows_p["kv_c"].append(u[:, o_kvc:o_kvc + kvc_w].reshape(batch, seq, 2, kv_c, HEAD_DIM))
        rows_p["logf"].append(lf[:, :h_b].reshape(batch, seq, h_b))
        yp = moe_sorted(hp, g_norm_ffn[l], w_r, b_r, w_eg16, w_eu16, w_ed16, l, n_groups, e_per_group)
        xp = hp

        if ys is None:
            xn_s = rmsnorm(xs, g_norm_mix[l], F32)
        else:
            xs, xn_s = add_rmsnorm(xs, ys, g_norm_mix[l], F32, write_sum=True)
        us = matmul_hp(xn_s, w_u32, l, name="in_proj_s")
        lf_s = logf_only(us, b_f_pad[l][None, :], o_f // LANES)[:, :h_b]
        q_of = lambda off, h: us[:, off:off + h * HEAD_DIM].reshape(dec_batch, h, HEAD_DIM)
        kv_of = lambda off, n_kv, part: us[:, off + part * n_kv * HEAD_DIM: off + (part + 1) * n_kv * HEAD_DIM].reshape(
            dec_batch, n_kv, HEAD_DIM)
        o_as = decode_attention_a(q_of(o_qa, h_a), pages_a, l, page_table, kv_a)
        o_bs = decode_attention_b(q_of(o_qb, h_b), kv_of(o_kvb, kv_b, 0), kv_of(o_kvb, kv_b, 1), lf_s,
                                  pages_b, lf_pages, l, page_table, kv_b)
        o_cs = decode_attention_c(q_of(o_qc, h_c), kv_of(o_kvc, kv_c, 0), kv_of(o_kvc, kv_c, 1), slopes,
                                  pages_c, l, page_table, kv_c)
        merged_s = branch_merge_hp(xn_s, o_as, o_bs, o_cs, w_g32, w_br_a, w_br_b, w_br_c, l)
        hs = matmul_hp(merged_s, w_o, l, resid=xs, name="out_proj_s")
        rows_s["kv_a"].append(us[:, o_kva:o_kva + kva_w].reshape(dec_batch, 1, 2, kv_a, HEAD_DIM))
        rows_s["kv_b"].append(us[:, o_kvb:o_kvb + kvb_w].reshape(dec_batch, 1, 2, kv_b, HEAD_DIM))
        rows_s["kv_c"].append(us[:, o_kvc:o_kvc + kvc_w].reshape(dec_batch, 1, 2, kv_c, HEAD_DIM))
        rows_s["logf"].append(lf_s.reshape(dec_batch, 1, h_b))
        pk_s = moe_router(hs, g_norm_ffn[l], w_r, b_r, l, n_groups, e_per_group)
        ys = moe_experts_hp(pk_s[:, :d_model], pk_s[:, d_model:], w_exp_gate, w_exp_up, w_exp_down, l, n_groups)
        xs = hs

    y_prompt = add_rmsnorm(xp, yp, g_norm_final, F32, write_sum=False).reshape(batch, seq, d_model)
    y_sample = add_rmsnorm(xs, ys, g_norm_final, F32, write_sum=False).reshape(dec_batch, 1, d_model)
    st = lambda xs_: jnp.stack(xs_, axis=0)
    return (y_prompt, y_sample, st(rows_p["kv_a"]), st(rows_p["kv_b"]), st(rows_p["logf"]), st(rows_p["kv_c"]),
            st(rows_s["kv_a"]), st(rows_s["kv_b"]), st(rows_s["logf"]), st(rows_s["kv_c"]))
```

```python
import functools

import jax
import jax.numpy as jnp
from jax import lax
from jax.experimental import pallas as pl
from jax.experimental.pallas import tpu as pltpu

F32 = jnp.float32
BF16 = jnp.bfloat16

HEAD_DIM = 128
LANES = 128
SUBLANES = 8
MXU_COLS = 256
VMEM_LIMIT_BIG = 56 * 1024 * 1024
VMEM_LIMIT_MID = 40 * 1024 * 1024
Q_TILE = 256
MOBA_BLOCK = 256
MOBA_TOPK = 3
ALIBI_MAX_BIAS = 8.0
TOPK_IN_GROUP = 2
RMS_EPS = 1e-6
NEG_BIG = -1e30
DEC_PAGES_PER_STEP = 16
MOE_TILE = 512
GATHER_ROWS = 128


def _cparams(sem, vmem=None):
    return pltpu.CompilerParams(dimension_semantics=sem, vmem_limit_bytes=vmem)


def _pick_tile(n, cap, mult):
    if n <= cap:
        return n
    best = None
    t = mult
    while t <= cap:
        if n % t == 0:
            best = t
        t += mult
    assert best is not None, (n, cap, mult)
    return best


def _split3(x):
    hi = x.astype(BF16)
    r = x - hi.astype(F32)
    mid = r.astype(BF16)
    lo = (r - mid.astype(F32)).astype(BF16)
    return hi, mid, lo


def _dot(a, b):
    return jnp.dot(a, b, preferred_element_type=F32)


def _dot_nt(a, b):
    return lax.dot_general(a, b, (((1,), (1,)), ((), ())), preferred_element_type=F32)


def _dot3(x, u):
    hi, mid, lo = _split3(x)
    return _dot(hi, u) + _dot(mid, u) + _dot(lo, u)


def _softplus(z):
    return jnp.maximum(z, 0.0) + jnp.log1p(jnp.exp(-jnp.abs(z)))


def _log_sigmoid(z):
    return jnp.minimum(z, 0.0) - jnp.log1p(jnp.exp(-jnp.abs(z)))


def _iota(shape, dim):
    return lax.broadcasted_iota(jnp.int32, shape, dim)


def _rowsum(x):
    return jnp.sum(x, axis=1, keepdims=True)


def _rowmax(x):
    return jnp.max(x, axis=1, keepdims=True)


def _rms_kernel(x_ref, g_ref, o_ref):
    x = x_ref[...]
    ms = jnp.mean(x * x, axis=-1, keepdims=True)
    o_ref[...] = ((x * lax.rsqrt(ms + RMS_EPS)) * g_ref[...]).astype(o_ref.dtype)


def rmsnorm(x, g, out_dtype):
    m, d = x.shape
    tm = _pick_tile(m, 256, 8)
    return pl.pallas_call(
        _rms_kernel,
        out_shape=jax.ShapeDtypeStruct((m, d), out_dtype),
        grid=(m // tm,),
        in_specs=[pl.BlockSpec((tm, d), lambda i: (i, 0)),
                  pl.BlockSpec((1, d), lambda i: (0, 0))],
        out_specs=pl.BlockSpec((tm, d), lambda i: (i, 0)),
        compiler_params=_cparams(("parallel",)),
        name="rmsnorm",
    )(x, g.reshape(1, d))


def _add_rms_kernel(a_ref, b_ref, g_ref, *out_refs):
    x = a_ref[...] + b_ref[...]
    ms = jnp.mean(x * x, axis=-1, keepdims=True)
    y = (x * lax.rsqrt(ms + RMS_EPS)) * g_ref[...]
    out_refs[-1][...] = y.astype(out_refs[-1].dtype)
    if len(out_refs) == 2:
        out_refs[0][...] = x


def add_rmsnorm(a, b, g, out_dtype, write_sum):
    m, d = a.shape
    tm = _pick_tile(m, 256, 8)
    row = pl.BlockSpec((tm, d), lambda i: (i, 0))
    out_shape = [jax.ShapeDtypeStruct((m, d), out_dtype)]
    out_specs = [row]
    if write_sum:
        out_shape.insert(0, jax.ShapeDtypeStruct((m, d), F32))
        out_specs.insert(0, row)
    res = pl.pallas_call(
        _add_rms_kernel,
        out_shape=tuple(out_shape),
        grid=(m // tm,),
        in_specs=[row, row, pl.BlockSpec((1, d), lambda i: (0, 0))],
        out_specs=tuple(out_specs),
        compiler_params=_cparams(("parallel",), VMEM_LIMIT_MID),
        name="add_rmsnorm",
    )(a, b, g.reshape(1, d))
    return res if write_sum else res[0]


def _mm_kernel(x_ref, w_ref, o_ref):
    o_ref[...] = _dot(x_ref[...], w_ref[...]).astype(o_ref.dtype)


def _mm_resid_kernel(x_ref, w_ref, r_ref, o_ref):
    o_ref[...] = (r_ref[...] + _dot(x_ref[...], w_ref[...])).astype(o_ref.dtype)


def matmul(x, w, layer, out_dtype, resid=None, name="matmul"):
    m, k = x.shape
    n = w.shape[2]
    tm = _pick_tile(m, 1024, 8)
    tn = _pick_tile(n, 1024, MXU_COLS)
    in_specs = [pl.BlockSpec((tm, k), lambda i, j: (i, 0)),
                pl.BlockSpec((None, k, tn), lambda i, j: (layer, 0, j))]
    args = [x, w]
    kern = _mm_kernel
    if resid is not None:
        in_specs.append(pl.BlockSpec((tm, tn), lambda i, j: (i, j)))
        args.append(resid)
        kern = _mm_resid_kernel
    return pl.pallas_call(
        kern,
        out_shape=jax.ShapeDtypeStruct((m, n), out_dtype),
        grid=(m // tm, n // tn),
        in_specs=in_specs,
        out_specs=pl.BlockSpec((tm, tn), lambda i, j: (i, j)),
        compiler_params=_cparams(("parallel", "parallel"), VMEM_LIMIT_BIG),
        name=name,
    )(*args)


def _logf_kernel(u_ref, b_ref, lf_ref):
    lf_ref[...] = _log_sigmoid(u_ref[...] + b_ref[...])


def logf_only(u, b_f_pad, col_block):
    m = u.shape[0]
    return pl.pallas_call(
        _logf_kernel,
        out_shape=jax.ShapeDtypeStruct((m, LANES), F32),
        grid=(1,),
        in_specs=[pl.BlockSpec((m, LANES), lambda i: (0, col_block)),
                  pl.BlockSpec((1, LANES), lambda i: (0, 0))],
        out_specs=pl.BlockSpec((m, LANES), lambda i: (0, 0)),
        name="logf_decode",
    )(u, b_f_pad)


def _logf_cum_kernel(u_ref, b_ref, lf_ref, c_ref, *, chunk):
    t = u_ref.shape[0]
    lf = _log_sigmoid(u_ref[...] + b_ref[...])
    lf_ref[...] = lf
    tri = (_iota((chunk, chunk), 0) >= _iota((chunk, chunk), 1)).astype(BF16)
    carry = jnp.zeros((1, LANES), F32)
    for n in range(t // chunk):
        hi, mid, lo = _split3(lf[n * chunk:(n + 1) * chunk])
        c = _dot(tri, hi) + _dot(tri, mid) + _dot(tri, lo) + carry
        c_ref[n * chunk:(n + 1) * chunk, :] = c
        carry = c[chunk - 1:chunk, :]


def logf_cumsum(u, b_f_pad, batch, seq, col_block):
    chunk = _pick_tile(seq, 256, 8)
    kern = functools.partial(_logf_cum_kernel, chunk=chunk)
    return pl.pallas_call(
        kern,
        out_shape=(jax.ShapeDtypeStruct((batch * seq, LANES), F32),
                   jax.ShapeDtypeStruct((batch * seq, LANES), F32)),
        grid=(batch,),
        in_specs=[pl.BlockSpec((seq, LANES), lambda b: (b, col_block)),
                  pl.BlockSpec((1, LANES), lambda b: (0, 0))],
        out_specs=(pl.BlockSpec((seq, LANES), lambda b: (b, 0)),
                   pl.BlockSpec((seq, LANES), lambda b: (b, 0))),
        compiler_params=_cparams(("parallel",)),
        name="logf_cumsum",
    )(u, b_f_pad)


def _colmax(x):
    return jnp.max(x, axis=0, keepdims=True)


def _colsum(x):
    return jnp.sum(x, axis=0, keepdims=True)


def _stage_kv(k_ref, v_ref, kb_sc, vt_sc, tq):
    for n in range(k_ref.shape[0] // tq):
        rows = slice(n * tq, (n + 1) * tq)
        kb_sc[n] = k_ref[rows, :].astype(BF16)
        vt_sc[n] = v_ref[rows, :].T.astype(BF16)


def _q_t(q_ref, g):
    return q_ref[:, g * HEAD_DIM:(g + 1) * HEAD_DIM].T.astype(BF16)


def _key_pos(tq):
    return _iota((tq, tq), 0), _iota((tq, tq), 1)


def _stack_heads(q, g_heads):
    return jnp.concatenate([q[:, g * HEAD_DIM:(g + 1) * HEAD_DIM] for g in range(g_heads)], axis=0)


def _unstack_heads(o, g_heads, tq):
    return jnp.concatenate([o[g * tq:(g + 1) * tq] for g in range(g_heads)], axis=1)


def _tile_pos(g_heads, tq):
    row_t = jnp.concatenate([_iota((tq, tq), 0)] * g_heads, axis=0)
    col = _iota((g_heads * tq, tq), 1)
    return row_t, col


def _attn_a_kernel(q_ref, k_ref, v_ref, o_ref, *, tq, g_heads, scale):
    i = pl.program_id(2)
    rows = g_heads * tq
    qs = _stack_heads(q_ref[...], g_heads).astype(BF16)
    row_t, col = _tile_pos(g_heads, tq)
    suffix = (_iota((tq, tq), 0) >= _iota((tq, tq), 1)).astype(BF16)

    def block(j, carry, masked):
        o, c = carry
        start = pl.multiple_of(j * tq, tq)
        k = k_ref[pl.ds(start, tq), :].astype(BF16)
        v = v_ref[pl.ds(start, tq), :].astype(BF16)
        z = _dot_nt(qs, k) * scale
        sp = _softplus(z)
        if masked:
            reads = col < row_t
            sp = jnp.where(reads, sp, 0.0)
        hi = sp.astype(BF16)
        lo = (sp - hi.astype(F32)).astype(BF16)
        s_incl = c + _dot(hi, suffix) + _dot(lo, suffix)
        w = jnp.exp(z - s_incl)
        if masked:
            w = jnp.where(reads, w, 0.0)
        o = o + _dot(w.astype(BF16), v)
        return o, s_incl[:, 0:1]

    carry = (jnp.zeros((rows, HEAD_DIM), F32), jnp.zeros((rows, 1), F32))
    carry = block(i, carry, True)
    carry = lax.fori_loop(0, i, lambda s, cr: block(i - 1 - s, cr, False), carry)
    o_ref[...] = _unstack_heads(carry[0], g_heads, tq).astype(o_ref.dtype)


def _attn_b_kernel(q_ref, k_ref, v_ref, c_ref, o_ref, *, tq, g_heads, scale):
    i = pl.program_id(2)
    rows = g_heads * tq
    qs = _stack_heads(q_ref[...], g_heads).astype(BF16)
    row_t, col = _tile_pos(g_heads, tq)
    eye = _iota((tq, tq), 0) == _iota((tq, tq), 1)
    cq = jnp.concatenate(
        [jnp.sum(jnp.where(eye, c_ref[g, i], 0.0), axis=1, keepdims=True) for g in range(g_heads)],
        axis=0)

    def block(j, carry, masked):
        m, l, acc = carry
        start = pl.multiple_of(j * tq, tq)
        k = k_ref[pl.ds(start, tq), :].astype(BF16)
        v = v_ref[pl.ds(start, tq), :].astype(BF16)
        z = _dot_nt(qs, k) * scale
        ck = jnp.concatenate([jnp.broadcast_to(c_ref[g, j], (tq, tq)) for g in range(g_heads)], axis=0)
        z = (z + cq) - ck
        if masked:
            z = jnp.where(col <= row_t, z, NEG_BIG)
        m_new = jnp.maximum(m, jnp.max(z, axis=1, keepdims=True))
        alpha = jnp.exp(m - m_new)
        p = jnp.exp(z - m_new)
        l = alpha * l + jnp.sum(p, axis=1, keepdims=True)
        acc = alpha * acc + _dot(p.astype(BF16), v)
        return m_new, l, acc

    carry = (jnp.full((rows, 1), NEG_BIG, F32), jnp.zeros((rows, 1), F32),
             jnp.zeros((rows, HEAD_DIM), F32))
    carry = block(i, carry, True)
    carry = lax.fori_loop(0, i, lambda s, cr: block(s, cr, False), carry)
    o = carry[2] / carry[1]
    o_ref[...] = _unstack_heads(o, g_heads, tq).astype(o_ref.dtype)


def _topk_rank_lanes(gate, lane, n_valid_lanes):
    rank = jnp.zeros(gate.shape, F32)
    for n in range(n_valid_lanes):
        gn = gate[:, n:n + 1]
        beats = (gate > gn) | ((gate == gn) & (lane < n))
        cnt = jnp.sum(beats.astype(F32), axis=1, keepdims=True)
        rank = jnp.where(lane == n, cnt, rank)
    return rank


def _attn_c_kernel(slopes_ref, q_ref, k_ref, v_ref, o_ref, kb_sc, vt_sc, km_sc, m_sc, l_sc, acc_sc,
                   *, tq, g_heads, scale, nb):
    h = pl.program_id(1)
    i = pl.program_id(2)
    nbp = km_sc.shape[0]

    @pl.when(i == 0)
    def _():
        _stage_kv(k_ref, v_ref, kb_sc, vt_sc, tq)
        km_sc[...] = jnp.zeros(km_sc.shape, F32)
        for n in range(nb):
            km_sc[n:n + 1, :] = jnp.sum(k_ref[n * tq:(n + 1) * tq, :], axis=0, keepdims=True) * (1.0 / tq)

    s_pos, t_pos = _key_pos(tq)
    rel = (t_pos - s_pos).astype(F32)
    blk = _iota((nbp, tq), 0)
    km = km_sc[...].astype(BF16)
    qts = [_q_t(q_ref, g) for g in range(g_heads)]
    slopes = [slopes_ref[h * g_heads + g] for g in range(g_heads)]
    sels = []
    for g in range(g_heads):
        gate = jnp.where(blk < i, _dot(km, qts[g]), -jnp.inf)
        sel_g = []
        for n in range(nb):
            gn = gate[n:n + 1, :]
            beats = (gate > gn) | ((gate == gn) & (blk < n))
            sel_g.append(_colsum(beats.astype(F32)) < float(MOBA_TOPK))
        sels.append(sel_g)
        zt = _dot(kb_sc[i], qts[g]) * scale - slopes[g] * rel
        zt = jnp.where(s_pos <= t_pos, zt, NEG_BIG)
        m0 = _colmax(zt)
        p = jnp.exp(zt - m0)
        m_sc[g] = m0
        l_sc[g] = _colsum(p)
        acc_sc[g] = _dot(vt_sc[i], p.astype(BF16))

    for n in range(nb):
        @pl.when(n < i)
        def _(n=n):
            kb = kb_sc[n]
            vt = vt_sc[n]
            dist = rel + ((i - n) * tq).astype(F32)
            for g in range(g_heads):
                take = sels[g][n]
                zt = jnp.where(take, _dot(kb, qts[g]) * scale - slopes[g] * dist, NEG_BIG)
                m_old = m_sc[g]
                m_new = jnp.maximum(m_old, _colmax(zt))
                alpha = jnp.exp(m_old - m_new)
                p = jnp.where(take, jnp.exp(zt - m_new), 0.0)
                m_sc[g] = m_new
                l_sc[g] = alpha * l_sc[g] + _colsum(p)
                acc_sc[g] = alpha * acc_sc[g] + _dot(vt, p.astype(BF16))

    for g in range(g_heads):
        o_ref[:, g * HEAD_DIM:(g + 1) * HEAD_DIM] = (acc_sc[g] / l_sc[g]).T.astype(o_ref.dtype)


def prompt_attention(kind, u, batch, seq, n_q_heads, n_kv, q_off, kv_off, extra=None):
    g_heads = n_q_heads // n_kv
    tq = Q_TILE
    assert seq % tq == 0 and q_off % (g_heads * HEAD_DIM) == 0 and kv_off % HEAD_DIM == 0
    nq = seq // tq
    qw = g_heads * HEAD_DIM
    scale = HEAD_DIM ** -0.5
    q_spec = pl.BlockSpec((tq, qw), lambda b, h, i, *_: (b * nq + i, q_off // qw + h))
    k_spec = pl.BlockSpec((seq, HEAD_DIM), lambda b, h, i, *_: (b, kv_off // HEAD_DIM + h))
    v_spec = pl.BlockSpec((seq, HEAD_DIM), lambda b, h, i, *_: (b, kv_off // HEAD_DIM + n_kv + h))
    o_spec = pl.BlockSpec((tq, qw), lambda b, h, i, *_: (b * nq + i, h))
    out_shape = jax.ShapeDtypeStruct((batch * seq, n_q_heads * HEAD_DIM), BF16)
    grid = (batch, n_kv, nq)
    sem = ("parallel", "parallel", "arbitrary")
    if kind == "a":
        kern = functools.partial(_attn_a_kernel, tq=tq, g_heads=g_heads, scale=scale)
        return pl.pallas_call(
            kern, out_shape=out_shape, grid=grid,
            in_specs=[q_spec, k_spec, v_spec], out_specs=o_spec,
            compiler_params=_cparams(sem, VMEM_LIMIT_MID), name="attn_stickbreak",
        )(u, u, u)
    if kind == "b":
        c_rows = extra
        c_spec = pl.BlockSpec((None, g_heads, nq, 1, tq), lambda b, h, i: (b, h, 0, 0, 0))
        kern = functools.partial(_attn_b_kernel, tq=tq, g_heads=g_heads, scale=scale)
        return pl.pallas_call(
            kern, out_shape=out_shape, grid=grid,
            in_specs=[q_spec, k_spec, v_spec, c_spec], out_specs=o_spec,
            compiler_params=_cparams(sem, VMEM_LIMIT_MID), name="attn_forget",
        )(u, u, u, c_rows)
    slopes = extra
    nb = seq // MOBA_BLOCK
    assert tq == MOBA_BLOCK
    nbp = -(-nb // SUBLANES) * SUBLANES
    staged = [pltpu.VMEM((nq, tq, HEAD_DIM), BF16), pltpu.VMEM((nq, HEAD_DIM, tq), BF16)]
    kern = functools.partial(_attn_c_kernel, tq=tq, g_heads=g_heads, scale=scale, nb=nb)
    return pl.pallas_call(
        kern, out_shape=out_shape,
        grid_spec=pltpu.PrefetchScalarGridSpec(
            num_scalar_prefetch=1, grid=grid,
            in_specs=[q_spec, k_spec, v_spec], out_specs=o_spec,
            scratch_shapes=staged + [pltpu.VMEM((nbp, HEAD_DIM), F32), pltpu.VMEM((g_heads, 1, tq), F32),
                                     pltpu.VMEM((g_heads, 1, tq), F32), pltpu.VMEM((g_heads, HEAD_DIM, tq), F32)]),
        compiler_params=_cparams(sem, VMEM_LIMIT_MID), name="attn_moba",
    )(slopes, u, u, u)


def _head_major(n_kv):
    return n_kv % 4 != 0


def cache_pages(cache):
    depth, n_pool, page, two, n_kv, d = cache.shape
    if _head_major(n_kv):
        cache = jnp.transpose(cache, (0, 1, 2, 4, 3, 5))
    return cache.reshape(depth, n_pool, page * two * n_kv, d)


def _page_kv(page_ref, h, n_kv, page, which):
    row = 2 * h + which if _head_major(n_kv) else which * n_kv + h
    return page_ref[pl.ds(row, page, stride=2 * n_kv), :]


def _lanes(x):
    return jnp.broadcast_to(x, (x.shape[0], LANES))


def _split2(x):
    hi = x.astype(BF16)
    return hi, (x - hi.astype(F32)).astype(BF16)


def _row_parts(x):
    hi, mid, lo = _split3(x)
    return jnp.concatenate([hi, mid, lo], axis=0), jnp.concatenate([hi, mid], axis=0)


def _fold(a, rows):
    out = a[:rows]
    for i in range(1, a.shape[0] // rows):
        out = out + a[i * rows:(i + 1) * rows]
    return out


def _mm_hp(x3, x2, w):
    rows = x2.shape[0] // 2
    wh, wm = _split2(w)
    return _fold(_dot(x3, wh), rows) + _fold(_dot(x2, wm), rows)


def _qk_hp(q3, q2, k):
    rows = q2.shape[0] // 2
    kh, km = _split2(k)
    return _fold(_dot_nt(q3, kh), rows) + _fold(_dot_nt(q2, km), rows)


def _pv_hp(p, v):
    rows = p.shape[0]
    ph, pm = _split2(p)
    vh, vm = _split2(v)
    return _fold(_dot(jnp.concatenate([ph, pm], axis=0), vh), rows) + _dot(ph, vm)


def _page_keys(page_ref, n_kv, page):
    return jnp.concatenate([_page_kv(page_ref, h, n_kv, page, 0) for h in range(n_kv)], axis=0)


def _page_values(page_ref, n_kv, page):
    return jnp.concatenate([_page_kv(page_ref, h, n_kv, page, 1) for h in range(n_kv)], axis=1)


def _own_head(x, n_kv, g_heads, width):
    row = _iota((x.shape[0], width), 0)
    out = jnp.zeros((x.shape[0], width), F32)
    for h in range(n_kv):
        mine = (row >= h * g_heads) & (row < (h + 1) * g_heads)
        out = out + jnp.where(mine, x[:, h * width:(h + 1) * width], 0.0)
    return out


def _scores(q3, q2, page_ref, n_kv, g_heads, page, scale):
    return _own_head(_qk_hp(q3, q2, _page_keys(page_ref, n_kv, page)), n_kv, g_heads, page) * scale


def _new_token_score(q, knew_ref, n_kv, g_heads, scale):
    row = _iota((q.shape[0], 1), 0)
    out = jnp.zeros((q.shape[0], 1), F32)
    for h in range(n_kv):
        mine = (row >= h * g_heads) & (row < (h + 1) * g_heads)
        out = out + jnp.where(mine, _rowsum(q * knew_ref[h]), 0.0)
    return out * scale


def _dec_a_kernel(pt_ref, q_ref, *rest, n_pages_step, n_kv, g_heads, page, scale):
    page_refs = rest[:n_pages_step]
    o_ref, acc_sc, c_sc = rest[n_pages_step:]
    j = pl.program_id(1)
    rows = q_ref.shape[0]

    @pl.when(j == 0)
    def _():
        acc_sc[...] = jnp.zeros(acc_sc.shape, F32)
        c_sc[...] = jnp.zeros(c_sc.shape, F32)

    q3, q2 = _row_parts(q_ref[...])
    suffix = (_iota((page, page), 0) >= _iota((page, page), 1)).astype(BF16)
    zs = [_scores(q3, q2, page_refs[r], n_kv, g_heads, page, scale) for r in range(n_pages_step)]
    sps = [_split2(_softplus(z)) for z in zs]
    stacked = jnp.concatenate([hi for hi, _ in sps] + [lo for _, lo in sps], axis=0)
    sums = _dot(stacked, suffix)
    c = c_sc[:, 0:1]
    acc = acc_sc[...]
    for r in range(n_pages_step):
        local = sums[r * rows:(r + 1) * rows] + sums[(n_pages_step + r) * rows:(n_pages_step + r + 1) * rows]
        s_incl = c + local
        acc = acc + _pv_hp(jnp.exp(zs[r] - s_incl), _page_values(page_refs[r], n_kv, page))
        c = s_incl[:, 0:1]
    acc_sc[...] = acc
    c_sc[...] = _lanes(c)

    @pl.when(j == pl.num_programs(1) - 1)
    def _():
        o_ref[...] = _own_head(acc, n_kv, g_heads, HEAD_DIM)


def _dec_b_kernel(pt_ref, q_ref, knew_ref, vnew_ref, lfnew_ref, *rest, n_pages_step, n_kv, g_heads, page, scale):
    page_refs = rest[:n_pages_step]
    lf_refs = rest[n_pages_step:2 * n_pages_step]
    o_ref, m_sc, l_sc, acc_sc, cb_sc = rest[2 * n_pages_step:]
    j = pl.program_id(1)
    q = q_ref[...]
    rows = q.shape[0]

    @pl.when(j == 0)
    def _():
        m_sc[...] = _lanes(_new_token_score(q, knew_ref, n_kv, g_heads, scale))
        l_sc[...] = jnp.ones(l_sc.shape, F32)
        acc_sc[...] = jnp.broadcast_to(vnew_ref[...], acc_sc.shape)
        cb_sc[...] = lfnew_ref[...]

    q3, q2 = _row_parts(q)
    after = (_iota((page, page), 0) > _iota((page, page), 1)).astype(BF16)
    lfs = [lf_refs[r][...] for r in range(n_pages_step)]
    later_all = _dot3(jnp.concatenate(lfs, axis=0), after)
    cb = cb_sc[:, 0:1]
    zs = []
    for r in range(n_pages_step):
        later = later_all[r * rows:(r + 1) * rows]
        z = _scores(q3, q2, page_refs[r], n_kv, g_heads, page, scale)
        zs.append(z + (cb + later))
        cb = cb + (later[:, 0:1] + lfs[r][:, 0:1])
    m_old = m_sc[:, 0:1]
    m_new = jnp.maximum(m_old, _rowmax(functools.reduce(jnp.maximum, zs)))
    alpha = jnp.exp(m_old - m_new)
    ps = [jnp.exp(z - m_new) for z in zs]
    l_new = alpha * l_sc[:, 0:1] + _rowsum(functools.reduce(jnp.add, ps))
    acc = alpha * acc_sc[...]
    for r in range(n_pages_step):
        acc = acc + _pv_hp(ps[r], _page_values(page_refs[r], n_kv, page))
    m_sc[...] = _lanes(m_new)
    l_sc[...] = _lanes(l_new)
    acc_sc[...] = acc
    cb_sc[...] = _lanes(cb)

    @pl.when(j == pl.num_programs(1) - 1)
    def _():
        o_ref[...] = _own_head(acc, n_kv, g_heads, HEAD_DIM) / l_new


def _dec_c_kernel(pt_ref, q_ref, knew_ref, vnew_ref, slope_ref, *rest,
                  n_blocks_step, n_pages_blk, n_kv, g_heads, page, scale, past_len, n_blocks):
    n_page_refs = n_blocks_step * n_pages_blk
    page_refs = rest[:n_page_refs]
    o_ref, part_sc, g_sc, m_sc, l_sc = rest[n_page_refs:]
    j = pl.program_id(1)
    q = q_ref[...]
    rows = q.shape[0]
    lane = _iota((rows, LANES), 1)
    n_keys = n_pages_blk * page

    @pl.when(j == 0)
    def _():
        g_sc[...] = jnp.zeros(g_sc.shape, F32)
        m_sc[...] = jnp.zeros(m_sc.shape, F32)
        l_sc[...] = jnp.zeros(l_sc.shape, F32)

    q3, q2 = _row_parts(q)
    slope = slope_ref[:, 0:1]
    row1 = _iota((rows, 1), 0)
    for bb in range(n_blocks_step):
        n = j * n_blocks_step + bb
        zs = []
        g_n = jnp.zeros((rows, 1), F32)
        refs = page_refs[bb * n_pages_blk:(bb + 1) * n_pages_blk]
        for r, page_ref in enumerate(refs):
            pos = n * n_keys + r * page + _iota((1, page), 1)
            dist = (past_len - pos).astype(F32)
            zs.append(_scores(q3, q2, page_ref, n_kv, g_heads, page, scale) - slope * dist)
        for h in range(n_kv):
            ksum = functools.reduce(jnp.add, [jnp.sum(_page_kv(pr, h, n_kv, page, 0), axis=0, keepdims=True)
                                              for pr in refs])
            mine = (row1 >= h * g_heads) & (row1 < (h + 1) * g_heads)
            g_n = g_n + jnp.where(mine, _rowsum(q * (ksum * (1.0 / n_keys))), 0.0)
        m_n = _rowmax(functools.reduce(jnp.maximum, zs))
        ps = [jnp.exp(z - m_n) for z in zs]
        l_n = _rowsum(functools.reduce(jnp.add, ps))
        part_sc[n] = functools.reduce(
            jnp.add, [_pv_hp(p, _page_values(pr, n_kv, page)) for p, pr in zip(ps, refs)])
        g_sc[...] = jnp.where(lane == n, g_n, g_sc[...])
        m_sc[...] = jnp.where(lane == n, m_n, m_sc[...])
        l_sc[...] = jnp.where(lane == n, l_n, l_sc[...])

    @pl.when(j == pl.num_programs(1) - 1)
    def _():
        gate = jnp.where(lane < n_blocks, g_sc[...], -jnp.inf)
        rank = _topk_rank_lanes(gate, lane, n_blocks)
        sel = (rank < float(MOBA_TOPK)) & (lane < n_blocks)
        z_own = _new_token_score(q, knew_ref, n_kv, g_heads, scale)
        m_star = jnp.maximum(z_own, _rowmax(jnp.where(sel, m_sc[...], NEG_BIG)))
        wts = jnp.where(sel, jnp.exp(m_sc[...] - m_star), 0.0)
        w_own = jnp.exp(z_own - m_star)
        l_tot = w_own + _rowsum(wts * l_sc[...])
        o = w_own * jnp.broadcast_to(vnew_ref[...], (rows, vnew_ref.shape[1]))
        for n in range(n_blocks):
            o = o + wts[:, n:n + 1] * part_sc[n]
        o_ref[...] = _own_head(o, n_kv, g_heads, HEAD_DIM) / l_tot


def _head_rows(n_q_heads):
    return -(-n_q_heads // SUBLANES) * SUBLANES


def _pad_heads(x):
    rows = _head_rows(x.shape[1])
    return jnp.pad(x, ((0, 0), (0, rows - x.shape[1])) + ((0, 0),) * (x.ndim - 2))


def _decode_call(kern, name, page_table, n_steps, rows, per_seq_inputs, shared_inputs, paged, scratch):
    s = page_table.shape[0]
    in_specs = [pl.BlockSpec((None,) + x.shape[1:], lambda b, j, pt, nd=x.ndim: (b,) + (0,) * (nd - 1))
                for x in per_seq_inputs]
    in_specs += [pl.BlockSpec(x.shape, lambda b, j, pt, nd=x.ndim: (0,) * nd) for x in shared_inputs]
    args = list(per_seq_inputs) + list(shared_inputs)
    for arr, layer, page_fn in paged:
        in_specs.append(pl.BlockSpec(
            (None, None) + arr.shape[2:],
            lambda b, j, pt, layer=layer, page_fn=page_fn: (layer, page_fn(b, j, pt), 0, 0)))
        args.append(arr)
    return pl.pallas_call(
        kern,
        out_shape=jax.ShapeDtypeStruct((s, rows, HEAD_DIM), F32),
        grid_spec=pltpu.PrefetchScalarGridSpec(
            num_scalar_prefetch=1, grid=(s, n_steps),
            in_specs=in_specs,
            out_specs=pl.BlockSpec((None, rows, HEAD_DIM), lambda b, j, pt: (b, 0, 0)),
            scratch_shapes=scratch),
        compiler_params=_cparams(("parallel", "arbitrary"), VMEM_LIMIT_BIG),
        name=name,
    )(page_table, *args)


def _state(rows):
    return pltpu.VMEM((rows, LANES), F32)


def decode_attention_a(q, pages, layer, page_table, n_kv):
    s, h, _ = q.shape
    rows = _head_rows(h)
    n_pages = page_table.shape[1]
    page = pages.shape[2] // (2 * n_kv)
    pps = _pick_tile(n_pages, DEC_PAGES_PER_STEP, 1)
    latest_first = lambda r: (lambda b, j, pt: pt[b, n_pages - 1 - (j * pps + r)])
    kern = functools.partial(_dec_a_kernel, n_pages_step=pps, n_kv=n_kv, g_heads=h // n_kv, page=page,
                             scale=HEAD_DIM ** -0.5)
    out = _decode_call(kern, "decode_stickbreak", page_table, n_pages // pps, rows,
                       [_pad_heads(q)], [],
                       [(pages, layer, latest_first(r)) for r in range(pps)],
                       [pltpu.VMEM((rows, n_kv * HEAD_DIM), F32), _state(rows)])
    return out[:, :h].reshape(s, h * HEAD_DIM)


def decode_attention_b(q, k_new, v_new, lf_new, pages, lf_pages, layer, page_table, n_kv):
    s, h, _ = q.shape
    rows = _head_rows(h)
    n_pages = page_table.shape[1]
    page = pages.shape[2] // (2 * n_kv)
    pps = _pick_tile(n_pages, DEC_PAGES_PER_STEP, 1)
    latest_first = lambda r: (lambda b, j, pt: pt[b, n_pages - 1 - (j * pps + r)])
    lfn = jnp.broadcast_to(_pad_heads(lf_new)[..., None], (s, rows, LANES))
    kern = functools.partial(_dec_b_kernel, n_pages_step=pps, n_kv=n_kv, g_heads=h // n_kv, page=page,
                             scale=HEAD_DIM ** -0.5)
    out = _decode_call(kern, "decode_forget", page_table, n_pages // pps, rows,
                       [_pad_heads(q), k_new[:, :, None, :], v_new.reshape(s, 1, n_kv * HEAD_DIM), lfn], [],
                       [(pages, layer, latest_first(r)) for r in range(pps)]
                       + [(lf_pages, layer, latest_first(r)) for r in range(pps)],
                       [_state(rows), _state(rows), pltpu.VMEM((rows, n_kv * HEAD_DIM), F32), _state(rows)])
    return out[:, :h].reshape(s, h * HEAD_DIM)


def decode_attention_c(q, k_new, v_new, slopes, pages, layer, page_table, n_kv):
    s, h, _ = q.shape
    rows = _head_rows(h)
    n_pages = page_table.shape[1]
    page = pages.shape[2] // (2 * n_kv)
    assert MOBA_BLOCK % page == 0
    ppb = MOBA_BLOCK // page
    assert n_pages % ppb == 0
    n_blocks = n_pages // ppb
    assert n_blocks <= LANES
    bps = _pick_tile(n_blocks, max(DEC_PAGES_PER_STEP // ppb, 1), 1)
    pps = bps * ppb
    slope_rows = jnp.broadcast_to(_pad_heads(slopes[None])[0][:, None], (rows, LANES))
    in_order = lambda r: (lambda b, j, pt: pt[b, j * pps + r])
    kern = functools.partial(_dec_c_kernel, n_blocks_step=bps, n_pages_blk=ppb, n_kv=n_kv, g_heads=h // n_kv,
                             page=page, scale=HEAD_DIM ** -0.5, past_len=n_pages * page, n_blocks=n_blocks)
    out = _decode_call(kern, "decode_moba", page_table, n_blocks // bps, rows,
                       [_pad_heads(q), k_new[:, :, None, :], v_new.reshape(s, 1, n_kv * HEAD_DIM)], [slope_rows],
                       [(pages, layer, in_order(r)) for r in range(pps)],
                       [pltpu.VMEM((n_blocks, rows, n_kv * HEAD_DIM), F32)] + [_state(rows)] * 3)
    return out[:, :h].reshape(s, h * HEAD_DIM)


def _mm_hp_kernel(x_ref, w_ref, *rest):
    x3, x2 = _row_parts(x_ref[...])
    out = _mm_hp(x3, x2, w_ref[...])
    if len(rest) == 2:
        out = out + rest[0][...]
    rest[-1][...] = out


def matmul_hp(x, w, layer, resid=None, name="matmul_hp"):
    m, k = x.shape
    n = w.shape[2]
    tn = _pick_tile(n, 512, MXU_COLS)
    in_specs = [pl.BlockSpec((m, k), lambda j: (0, 0)),
                pl.BlockSpec((None, k, tn), lambda j: (layer, 0, j))]
    args = [x, w]
    if resid is not None:
        in_specs.append(pl.BlockSpec((m, tn), lambda j: (0, j)))
        args.append(resid)
    return pl.pallas_call(
        _mm_hp_kernel,
        out_shape=jax.ShapeDtypeStruct((m, n), F32),
        grid=(n // tn,),
        in_specs=in_specs,
        out_specs=pl.BlockSpec((m, tn), lambda j: (0, j)),
        compiler_params=_cparams(("parallel",), VMEM_LIMIT_BIG),
        name=name,
    )(*args)


def _merge_hp_kernel(xn_ref, oa_ref, ob_ref, oc_ref, wg_ref, wa_ref, wb_ref, wc_ref, o_ref):
    x3, x2 = _row_parts(xn_ref[...])
    acc = None
    for br, (o_r, w_r) in enumerate(((oa_ref, wa_ref), (ob_ref, wb_ref), (oc_ref, wc_ref))):
        gate = jax.nn.sigmoid(_mm_hp(x3, x2, wg_ref[br]))
        o3, o2 = _row_parts(o_r[...])
        term = gate * _mm_hp(o3, o2, w_r[...])
        acc = term if acc is None else acc + term
    o_ref[...] = acc


def branch_merge_hp(xn, o_a, o_b, o_c, wg, w_a, w_b, w_c, layer):
    m, d = xn.shape
    tn = _pick_tile(d, 256, LANES)
    act = lambda a: pl.BlockSpec(a.shape, lambda j: (0, 0))
    wsp = lambda w: pl.BlockSpec((None, w.shape[1], tn), lambda j: (layer, 0, j))
    return pl.pallas_call(
        _merge_hp_kernel,
        out_shape=jax.ShapeDtypeStruct((m, d), F32),
        grid=(d // tn,),
        in_specs=[act(xn), act(o_a), act(o_b), act(o_c),
                  pl.BlockSpec((None, 3, d, tn), lambda j: (layer, 0, 0, j)),
                  wsp(w_a), wsp(w_b), wsp(w_c)],
        out_specs=pl.BlockSpec((m, tn), lambda j: (0, j)),
        compiler_params=_cparams(("parallel",), VMEM_LIMIT_BIG),
        name="branch_merge_hp",
    )(xn, o_a, o_b, o_c, wg, w_a, w_b, w_c)


def _experts_hp_kernel(eid_ref, ok_ref, x_ref, comb_ref, wg_ref, wu_ref, wd_ref, o_ref, *, n_groups):
    s = pl.program_id(0)
    f = pl.program_id(1)

    @pl.when((s == 0) & (f == 0))
    def _():
        o_ref[...] = jnp.zeros(o_ref.shape, F32)

    @pl.when(ok_ref[s] > 0)
    def _():
        x3, x2 = _row_parts(x_ref[...])
        lane = _iota(comb_ref.shape, 1)
        w_tok = _rowsum(jnp.where(lane == n_groups + eid_ref[s], comb_ref[...], 0.0))
        hg = _mm_hp(x3, x2, wg_ref[...])
        hu = _mm_hp(x3, x2, wu_ref[...])
        hid = (hg * jax.nn.sigmoid(hg)) * hu * w_tok
        h3, h2 = _row_parts(hid)
        o_ref[...] += _mm_hp(h3, h2, wd_ref[...])


def moe_experts_hp(xn, comb, w_eg, w_eu, w_ed, layer, n_groups):
    m, d = xn.shape
    n_experts, d_ff = w_eg.shape[1], w_eg.shape[3]
    tf = _pick_tile(d_ff, 256, LANES)
    nf = d_ff // tf
    active = jnp.any(comb[:, n_groups:n_groups + n_experts] > 0.0, axis=0)
    order = jnp.argsort(jnp.logical_not(active), stable=True).astype(jnp.int32)
    n_active = jnp.sum(active.astype(jnp.int32))
    step = jnp.arange(n_experts, dtype=jnp.int32)
    ok = (step < n_active).astype(jnp.int32)
    eid = order[jnp.minimum(step, jnp.maximum(n_active - 1, 0))]

    def fidx(s, f, eid_ref, ok_ref):
        return jnp.where(ok_ref[s] > 0, f, nf - 1)

    kern = functools.partial(_experts_hp_kernel, n_groups=n_groups)
    return pl.pallas_call(
        kern,
        out_shape=jax.ShapeDtypeStruct((m, d), F32),
        grid_spec=pltpu.PrefetchScalarGridSpec(
            num_scalar_prefetch=2, grid=(n_experts, nf),
            in_specs=[pl.BlockSpec((m, d), lambda s, f, e, k: (0, 0)),
                      pl.BlockSpec((m, LANES), lambda s, f, e, k: (0, 0)),
                      pl.BlockSpec((None, None, d, tf), lambda s, f, e, k: (layer, e[s], 0, fidx(s, f, e, k))),
                      pl.BlockSpec((None, None, d, tf), lambda s, f, e, k: (layer, e[s], 0, fidx(s, f, e, k))),
                      pl.BlockSpec((None, None, tf, d), lambda s, f, e, k: (layer, e[s], fidx(s, f, e, k), 0))],
            out_specs=pl.BlockSpec((m, d), lambda s, f, e, k: (0, 0))),
        compiler_params=_cparams(("arbitrary", "arbitrary"), VMEM_LIMIT_BIG),
        name="moe_experts_hp",
    )(eid, ok, xn, comb, w_eg, w_eu, w_ed)


def _merge_kernel(xn_ref, oa_ref, ob_ref, oc_ref, wg_ref, wa_ref, wb_ref, wc_ref, o_ref):
    xn = xn_ref[...]
    acc = None
    for br, (o_r, w_r) in enumerate(((oa_ref, wa_ref), (ob_ref, wb_ref), (oc_ref, wc_ref))):
        gate = jax.nn.sigmoid(_dot(xn, wg_ref[br]))
        term = gate * _dot(o_r[...], w_r[...])
        acc = term if acc is None else acc + term
    o_ref[...] = acc.astype(o_ref.dtype)


def branch_merge(xn, o_a, o_b, o_c, wg, w_a, w_b, w_c, layer):
    m, d = xn.shape
    tm = _pick_tile(m, 512, 8)
    tn = _pick_tile(d, 256, LANES)
    act = lambda a: pl.BlockSpec((tm, a.shape[1]), lambda i, j: (i, 0))
    wsp = lambda w: pl.BlockSpec((None, w.shape[1], tn), lambda i, j: (layer, 0, j))
    return pl.pallas_call(
        _merge_kernel,
        out_shape=jax.ShapeDtypeStruct((m, d), BF16),
        grid=(m // tm, d // tn),
        in_specs=[act(xn), act(o_a), act(o_b), act(o_c),
                  pl.BlockSpec((None, 3, d, tn), lambda i, j: (layer, 0, 0, j)),
                  wsp(w_a), wsp(w_b), wsp(w_c)],
        out_specs=pl.BlockSpec((tm, tn), lambda i, j: (i, j)),
        compiler_params=_cparams(("parallel", "parallel"), VMEM_LIMIT_BIG),
        name="branch_merge",
    )(xn, o_a, o_b, o_c, wg, w_a, w_b, w_c)


def _router_kernel(x_ref, g_ref, w_ref, b_ref, o_ref, *, n_groups, e_per_group):
    x = x_ref[...]
    d = x.shape[1]
    ms = jnp.mean(x * x, axis=-1, keepdims=True)
    xn = (x * lax.rsqrt(ms + RMS_EPS)) * g_ref[...]
    xh, xm, xl = _split3(xn)
    wh, wm, wl = _split3(w_ref[...])
    logits = (_dot(xh, wh) + (_dot(xh, wm) + _dot(xm, wh))
              + (_dot(xh, wl) + _dot(xm, wm) + _dot(xl, wh))) + b_ref[...]
    rows = logits.shape[0]
    lane = _iota((rows, LANES), 1).astype(F32)
    n_experts = n_groups * e_per_group
    is_g = lane < n_groups
    lg = jnp.where(is_g, logits, -jnp.inf)
    mg = jnp.max(lg, axis=1, keepdims=True)
    grp = jnp.min(jnp.where(is_g & (lg == mg), lane, float(LANES)), axis=1, keepdims=True)
    p_grp = 1.0 / jnp.sum(jnp.exp(lg - mg), axis=1, keepdims=True)
    e_lo = n_groups + grp * e_per_group
    in_grp = (lane >= e_lo) & (lane < e_lo + e_per_group)
    le = jnp.where(in_grp, logits, -jnp.inf)
    me = jnp.max(le, axis=1, keepdims=True)
    ee = jnp.exp(le - me)
    pe = ee / jnp.sum(ee, axis=1, keepdims=True)
    rank = jnp.zeros((rows, LANES), F32)
    for e in range(n_experts):
        ln = n_groups + e
        pv = pe[:, ln:ln + 1]
        beats = in_grp & ((pe > pv) | ((pe == pv) & (lane < ln)))
        cnt = jnp.sum(beats.astype(F32), axis=1, keepdims=True)
        rank = jnp.where(lane == ln, cnt, rank)
    top = in_grp & (rank < float(TOPK_IN_GROUP))
    top_p = jnp.where(top, pe, 0.0)
    top_p = top_p / jnp.sum(top_p, axis=1, keepdims=True)
    o_ref[:, :d] = xn
    o_ref[:, d:] = jnp.where(lane == n_groups + n_experts, grp, p_grp * top_p)


def moe_router(h, g, w_r, b_r, layer, n_groups, e_per_group):
    m, d = h.shape
    tm = _pick_tile(m, 256, 8)
    kern = functools.partial(_router_kernel, n_groups=n_groups, e_per_group=e_per_group)
    return pl.pallas_call(
        kern,
        out_shape=jax.ShapeDtypeStruct((m, d + LANES), F32),
        grid=(m // tm,),
        in_specs=[pl.BlockSpec((tm, d), lambda i: (i, 0)),
                  pl.BlockSpec((1, d), lambda i: (0, 0)),
                  pl.BlockSpec((None, d, LANES), lambda i: (layer, 0, 0)),
                  pl.BlockSpec((None, 1, LANES), lambda i: (layer, 0, 0))],
        out_specs=pl.BlockSpec((tm, d + LANES), lambda i: (i, 0)),
        compiler_params=_cparams(("parallel",), VMEM_LIMIT_MID),
        name="moe_router",
    )(h, g.reshape(1, d), w_r, b_r)


def _gather_kernel(idx_ref, src_hbm, *rest, rows, splits):
    out_refs = rest[:len(splits)]
    buf, sem = rest[len(splits):]
    i = pl.program_id(0)
    slot = lax.rem(i, 2)

    def row_copy(r, src_row, slot):
        return pltpu.make_async_copy(src_hbm.at[pl.ds(src_row, 1), :], buf.at[slot, pl.ds(r, 1), :], sem.at[slot])

    def start_tile(tile, slot):
        for r in range(rows):
            row_copy(r, idx_ref[tile * rows + r], slot).start(priority=r % 2)

    @pl.when(i == 0)
    def _():
        start_tile(0, 0)

    @pl.when(i + 1 < pl.num_programs(0))
    def _():
        start_tile(i + 1, 1 - slot)

    for r in range(rows):
        row_copy(r, 0, slot).wait()
    for o_ref, (lo, hi) in zip(out_refs, splits):
        o_ref[...] = buf[slot, :, lo:hi].astype(o_ref.dtype)


def gather_rows(src, idx, splits, dtypes):
    n_out = idx.shape[0]
    w = src.shape[1]
    rows = GATHER_ROWS
    assert n_out % rows == 0
    kern = functools.partial(_gather_kernel, rows=rows, splits=tuple(splits))
    res = pl.pallas_call(
        kern,
        out_shape=tuple(jax.ShapeDtypeStruct((n_out, hi - lo), dt) for (lo, hi), dt in zip(splits, dtypes)),
        grid_spec=pltpu.PrefetchScalarGridSpec(
            num_scalar_prefetch=1, grid=(n_out // rows,),
            in_specs=[pl.BlockSpec(memory_space=pl.ANY)],
            out_specs=tuple(pl.BlockSpec((rows, hi - lo), lambda i, idx_ref: (i, 0)) for lo, hi in splits),
            scratch_shapes=[pltpu.VMEM((2, rows, w), F32), pltpu.SemaphoreType.DMA((2,))]),
        compiler_params=_cparams(("arbitrary",)),
        name="gather_rows",
    )(idx, src)
    return res


def _group_experts_kernel(tg_ref, tv_ref, x_ref, comb_ref, wg_ref, wu_ref, wd_ref, o_ref,
                          *, n_groups, e_per_group):
    i = pl.program_id(0)
    e = pl.program_id(1)
    f = pl.program_id(2)

    @pl.when((e == 0) & (f == 0))
    def _():
        o_ref[...] = jnp.zeros(o_ref.shape, F32)

    @pl.when(tv_ref[i] > 0)
    def _():
        x = x_ref[...]
        lane = _iota(comb_ref.shape, 1)
        e_lane = n_groups + tg_ref[i] * e_per_group + e
        w_tok = jnp.sum(jnp.where(lane == e_lane, comb_ref[...], 0.0), axis=1, keepdims=True)
        hg = _dot(x, wg_ref[...])
        hu = _dot(x, wu_ref[...])
        hid = (hg * jax.nn.sigmoid(hg)) * hu * w_tok
        o_ref[...] += _dot(hid.astype(BF16), wd_ref[...])


def moe_group_experts(xs, combs, tile_group, tile_valid, w_eg, w_eu, w_ed, layer, n_groups, e_per_group):
    m, d = xs.shape
    d_ff = w_eg.shape[3]
    tm = MOE_TILE
    tf = _pick_tile(d_ff, 512, LANES)
    nf = d_ff // tf

    def w_idx(i, e, f, tg, tv):
        ok = tv[i] > 0
        return tg[i] * e_per_group + jnp.where(ok, e, e_per_group - 1), jnp.where(ok, f, nf - 1)

    def wcol(i, e, f, tg, tv):
        ex, fx = w_idx(i, e, f, tg, tv)
        return (layer, ex, 0, fx)

    def wrow(i, e, f, tg, tv):
        ex, fx = w_idx(i, e, f, tg, tv)
        return (layer, ex, fx, 0)

    kern = functools.partial(_group_experts_kernel, n_groups=n_groups, e_per_group=e_per_group)
    return pl.pallas_call(
        kern,
        out_shape=jax.ShapeDtypeStruct((m, d), F32),
        grid_spec=pltpu.PrefetchScalarGridSpec(
            num_scalar_prefetch=2, grid=(m // tm, e_per_group, nf),
            in_specs=[pl.BlockSpec((tm, d), lambda i, e, f, tg, tv: (i, 0)),
                      pl.BlockSpec((tm, LANES), lambda i, e, f, tg, tv: (i, 0)),
                      pl.BlockSpec((None, None, d, tf), wcol),
                      pl.BlockSpec((None, None, d, tf), wcol),
                      pl.BlockSpec((None, None, tf, d), wrow)],
            out_specs=pl.BlockSpec((tm, d), lambda i, e, f, tg, tv: (i, 0))),
        compiler_params=_cparams(("parallel", "arbitrary", "arbitrary"), VMEM_LIMIT_BIG),
        name="moe_group_experts",
    )(tile_group, tile_valid, xs, combs, w_eg, w_eu, w_ed)


def moe_sorted(h, g, w_r, b_r, w_eg, w_eu, w_ed, layer, n_groups, e_per_group):
    m, d = h.shape
    tm = MOE_TILE
    n_experts = n_groups * e_per_group
    packed = moe_router(h, g, w_r, b_r, layer, n_groups, e_per_group)
    grp = packed[:, d + n_groups + n_experts].astype(jnp.int32)
    onehot = (grp[:, None] == jnp.arange(n_groups)[None, :]).astype(jnp.int32)
    counts = jnp.sum(onehot, axis=0)
    rank = jnp.sum((jnp.cumsum(onehot, axis=0) - onehot) * onehot, axis=1)
    padded = ((counts + tm - 1) // tm) * tm
    ends = jnp.cumsum(padded)
    starts = ends - padded
    pos = starts[grp] + rank
    cap = m + n_groups * tm
    perm = jnp.zeros((cap,), jnp.int32).at[pos].set(jnp.arange(m, dtype=jnp.int32))
    tile_start = jnp.arange(cap // tm, dtype=jnp.int32) * tm
    tile_group = jnp.minimum(jnp.searchsorted(ends, tile_start, side="right"), n_groups - 1).astype(jnp.int32)
    tile_valid = (tile_start < ends[-1]).astype(jnp.int32)
    xs, combs = gather_rows(packed, perm, [(0, d), (d, d + LANES)], [BF16, F32])
    ys = moe_group_experts(xs, combs, tile_group, tile_valid, w_eg, w_eu, w_ed, layer, n_groups, e_per_group)
    (y,) = gather_rows(ys, pos, [(0, d)], [F32])
    return y


def kernel(x_prompt, x_sample, cache_kv_a, cache_kv_b, cache_logf_b, cache_kv_c, page_table, w_in, b_f, w_br_a, w_br_b, w_br_c, w_o, g_norm_mix, g_norm_ffn, w_router_group, b_router_group, w_router_expert, b_router_expert, w_exp_gate, w_exp_up, w_exp_down, g_norm_final):
    batch, seq, d_model = x_prompt.shape
    dec_batch, dec_seq, _ = x_sample.shape
    assert dec_seq == 1
    depth = w_in.shape[0]
    kv_a, kv_b, kv_c = cache_kv_a.shape[4], cache_kv_b.shape[4], cache_kv_c.shape[4]
    h_a, h_b, h_c = (w.shape[1] // HEAD_DIM for w in (w_br_a, w_br_b, w_br_c))
    assert cache_logf_b.shape[3] == h_b
    n_groups = w_router_group.shape[2]
    n_experts = w_router_expert.shape[2]
    e_per_group = n_experts // n_groups
    assert n_groups + n_experts < LANES

    qa_w, kva_w = h_a * HEAD_DIM, 2 * kv_a * HEAD_DIM
    qb_w, kvb_w = h_b * HEAD_DIM, 2 * kv_b * HEAD_DIM
    qc_w, kvc_w = h_c * HEAD_DIM, 2 * kv_c * HEAD_DIM
    f_pad = MXU_COLS
    o_qa = 0
    o_kva = o_qa + qa_w
    o_qb = o_kva + kva_w
    o_kvb = o_qb + qb_w
    o_f = o_kvb + kvb_w
    o_qc = o_f + f_pad
    o_kvc = o_qc + qc_w
    n_u = o_kvc + kvc_w
    src_f = o_f
    src_qc = src_f + h_b
    src_g = src_qc + qc_w + kvc_w
    assert w_in.shape[2] == src_g + 3 * d_model

    w_u32 = jnp.concatenate(
        [w_in[:, :, :src_qc], jnp.zeros((depth, d_model, f_pad - h_b), w_in.dtype), w_in[:, :, src_qc:src_g]],
        axis=2)
    w_g32 = jnp.transpose(w_in[:, :, src_g:].reshape(depth, d_model, 3, d_model), (0, 2, 1, 3))
    w_u, w_g = w_u32.astype(BF16), w_g32.astype(BF16)
    w_a16, w_b16, w_c16, w_o16 = (w.astype(BF16) for w in (w_br_a, w_br_b, w_br_c, w_o))
    w_eg16, w_eu16, w_ed16 = (w.astype(BF16) for w in (w_exp_gate, w_exp_up, w_exp_down))
    w_r = jnp.concatenate([w_router_group, w_router_expert], axis=2)
    w_r = jnp.pad(w_r, ((0, 0), (0, 0), (0, LANES - w_r.shape[2])))
    b_r = jnp.concatenate([b_router_group, b_router_expert], axis=1)
    b_r = jnp.pad(b_r, ((0, 0), (0, LANES - b_r.shape[1])))[:, None, :]
    b_f_pad = jnp.pad(b_f, ((0, 0), (0, LANES - h_b)))
    slopes = 2.0 ** (-ALIBI_MAX_BIAS * jnp.arange(1, h_c + 1, dtype=F32) / h_c)

    n_pool, page = cache_kv_a.shape[1], cache_kv_a.shape[2]
    pages_a, pages_b, pages_c = cache_pages(cache_kv_a), cache_pages(cache_kv_b), cache_pages(cache_kv_c)
    lf_pages = jnp.pad(jnp.transpose(cache_logf_b, (0, 1, 3, 2)),
                       ((0, 0), (0, 0), (0, _head_rows(h_b) - h_b), (0, 0)))

    n_p = batch * seq
    nq = seq // Q_TILE
    xp = x_prompt.reshape(n_p, d_model)
    xs = x_sample.reshape(dec_batch, d_model)
    rows_p = {k: [] for k in ("kv_a", "kv_b", "logf", "kv_c")}
    rows_s = {k: [] for k in ("kv_a", "kv_b", "logf", "kv_c")}

    yp = ys = None
    for l in range(depth):
        if yp is None:
            xn = rmsnorm(xp, g_norm_mix[l], BF16)
        else:
            xp, xn = add_rmsnorm(xp, yp, g_norm_mix[l], BF16, write_sum=True)
        u = matmul(xn, w_u, l, F32, name="in_proj")
        lf, cum = logf_cumsum(u, b_f_pad[l][None, :], batch, seq, o_f // LANES)
        c_rows = jnp.transpose(cum[:, :h_b].reshape(batch, seq, h_b), (0, 2, 1)).reshape(batch, h_b, nq, 1, Q_TILE)
        o_a = prompt_attention("a", u, batch, seq, h_a, kv_a, o_qa, o_kva)
        o_b = prompt_attention("b", u, batch, seq, h_b, kv_b, o_qb, o_kvb, c_rows)
        o_c = prompt_attention("c", u, batch, seq, h_c, kv_c, o_qc, o_kvc, slopes)
        merged = branch_merge(xn, o_a, o_b, o_c, w_g, w_a16, w_b16, w_c16, l)
        hp = matmul(merged, w_o16, l, F32, resid=xp, name="out_proj")
        rows_p["kv_a"].append(u[:, o_kva:o_kva + kva_w].reshape(batch, seq, 2, kv_a, HEAD_DIM))
        rows_p["kv_b"].append(u[:, o_kvb:o_kvb + kvb_w].reshape(batch, seq, 2, kv_b, HEAD_DIM))
        rows_p["kv_c"].append(u[:, o_kvc:o_kvc + kvc_w].reshape(batch, seq, 2, kv_c, HEAD_DIM))
        rows_p["logf"].append(lf[:, :h_b].reshape(batch, seq, h_b))
        yp = moe_sorted(hp, g_norm_ffn[l], w_r, b_r, w_eg16, w_eu16, w_ed16, l, n_groups, e_per_group)
        xp = hp

        if ys is None:
            xn_s = rmsnorm(xs, g_norm_mix[l], F32)
        else:
            xs, xn_s = add_rmsnorm(xs, ys, g_norm_mix[l], F32, write_sum=True)
        us = matmul_hp(xn_s, w_u32, l, name="in_proj_s")
        lf_s = logf_only(us, b_f_pad[l][None, :], o_f // LANES)[:, :h_b]
        q_of = lambda off, h: us[:, off:off + h * HEAD_DIM].reshape(dec_batch, h, HEAD_DIM)
        kv_of = lambda off, n_kv, part: us[:, off + part * n_kv * HEAD_DIM: off + (part + 1) * n_kv * HEAD_DIM].reshape(
            dec_batch, n_kv, HEAD_DIM)
        o_as = decode_attention_a(q_of(o_qa, h_a), pages_a, l, page_table, kv_a)
        o_bs = decode_attention_b(q_of(o_qb, h_b), kv_of(o_kvb, kv_b, 0), kv_of(o_kvb, kv_b, 1), lf_s,
                                  pages_b, lf_pages, l, page_table, kv_b)
        o_cs = decode_attention_c(q_of(o_qc, h_c), kv_of(o_kvc, kv_c, 0), kv_of(o_kvc, kv_c, 1), slopes,
                                  pages_c, l, page_table, kv_c)
        merged_s = branch_merge_hp(xn_s, o_as, o_bs, o_cs, w_g32, w_br_a, w_br_b, w_br_c, l)
        hs = matmul_hp(merged_s, w_o, l, resid=xs, name="out_proj_s")
        rows_s["kv_a"].append(us[:, o_kva:o_kva + kva_w].reshape(dec_batch, 1, 2, kv_a, HEAD_DIM))
        rows_s["kv_b"].append(us[:, o_kvb:o_kvb + kvb_w].reshape(dec_batch, 1, 2, kv_b, HEAD_DIM))
        rows_s["kv_c"].append(us[:, o_kvc:o_kvc + kvc_w].reshape(dec_batch, 1, 2, kv_c, HEAD_DIM))
        rows_s["logf"].append(lf_s.reshape(dec_batch, 1, h_b))
        pk_s = moe_router(hs, g_norm_ffn[l], w_r, b_r, l, n_groups, e_per_group)
        ys = moe_experts_hp(pk_s[:, :d_model], pk_s[:, d_model:], w_exp_gate, w_exp_up, w_exp_down, l, n_groups)
        xs = hs

    y_prompt = add_rmsnorm(xp, yp, g_norm_final, F32, write_sum=False).reshape(batch, seq, d_model)
    y_sample = add_rmsnorm(xs, ys, g_norm_final, F32, write_sum=False).reshape(dec_batch, 1, d_model)
    st = lambda xs_: jnp.stack(xs_, axis=0)
    return (y_prompt, y_sample, st(rows_p["kv_a"]), st(rows_p["kv_b"]), st(rows_p["logf"]), st(rows_p["kv_c"]),
            st(rows_s["kv_a"]), st(rows_s["kv_b"]), st(rows_s["logf"]), st(rows_s["kv_c"]))
```
